```python
import math
import jax
import jax.numpy as jnp
from jax import lax
import numpy as np

D_MODEL = 1024
BATCH = 1
SEQ = 16384
DEPTH = 4
DEC_BATCH = 16
DEC_SEQ = 16
PAST_LEN = 4096

CHUNK = 64
N_META = 16
N_MIXERS = 3
N_SSD_LAYERS = len(range(0, DEPTH, N_MIXERS))
N_DA_LAYERS = len(range(1, DEPTH, N_MIXERS))
N_RK_LAYERS = len(range(2, DEPTH, N_MIXERS))

SSD_INNER = 2 * D_MODEL
SSD_HEADDIM = 64
SSD_HEADS = SSD_INNER // SSD_HEADDIM
SSD_GROUPS = 8
SSD_HPG = SSD_HEADS // SSD_GROUPS
SSD_STATE = 128
SSD_CONV = 4
SSD_CONV_DIM = SSD_INNER + 2 * SSD_GROUPS * SSD_STATE
SSD_PROJ = SSD_INNER + SSD_CONV_DIM + SSD_HEADS
SSD_BLOCK = 64
SSD_NORM_EPS = 1e-5

DA_HD = 64
DA_HEADS = D_MODEL // (2 * DA_HD)
Q_BLOCK = 128
ROPE_THETA = 10000.0
SUBLN_EPS = 1e-5

RK_HD = 64
RK_HEADS = D_MODEL // RK_HD
RK_DECAY_LORA = 64
RK_AAA_LORA = 64
RK_GATE_LORA = 160
RK_GN_EPS = 64e-5

D_FF = 2816
FFN_CONV = 3

LN_EPS = 1e-5
DEEPNORM_ALPHA = (2 * DEPTH) ** 0.25
DEEPNORM_BETA = (8 * DEPTH) ** -0.25

SSD_NAMES = ('ssd_w_in', 'ssd_conv_w', 'ssd_conv_b', 'ssd_dt_bias', 'ssd_a_log', 'ssd_d', 'ssd_norm_w', 'ssd_w_out')
DA_NAMES = ('da_w_qkv', 'da_lambda', 'da_subln_g', 'da_w_o')
RK_NAMES = ('rk_mu', 'rk_w0', 'rk_w1', 'rk_w2', 'rk_a0', 'rk_a1', 'rk_a2', 'rk_g1', 'rk_g2', 'rk_k_k', 'rk_k_a',
            'rk_r_k', 'rk_w_r', 'rk_w_k', 'rk_w_v', 'rk_w_o', 'rk_lnx_g', 'rk_lnx_b')
FFN_NAMES = ('ffn_w_up', 'ffn_w_gate', 'ffn_conv_w', 'ffn_conv_b', 'ffn_w_down')
STATE_NAMES = ('k', 'v', 'ssm', 'ssd_conv', 'wkv', 'shift', 'ffn_conv')

kernel_name = 'hybrid_ssd_diffattn_rwkv7_convffn_stream_step'


def layer_norm(x, g, b):
    xf = x.astype(jnp.float32)
    mu = jnp.mean(xf, axis=-1, keepdims=True)
    var = jnp.mean(jnp.square(xf - mu), axis=-1, keepdims=True)
    return ((xf - mu) * lax.rsqrt(var + LN_EPS) * g + b).astype(x.dtype)


def rms_norm(x, g, eps):
    xf = x.astype(jnp.float32)
    return (xf * lax.rsqrt(jnp.mean(jnp.square(xf), axis=-1, keepdims=True) + eps) * g).astype(x.dtype)


def causal_dwconv(x, prev, w, b):
    width, l = w.shape[0], x.shape[1]
    xp = jnp.concatenate([prev.astype(x.dtype), x], axis=1)
    out = b
    for j in range(width):
        out = out + xp[:, j:j + l] * w[j]
    return out, xp[:, -(width - 1):]


def rotary(x, pos):
    half = x.shape[-1] // 2
    inv = ROPE_THETA ** (-jnp.arange(half, dtype=jnp.float32) / half)
    ang = pos.astype(jnp.float32)[:, None] * inv[None, :]
    shape = (1, x.shape[1]) + (1,) * (x.ndim - 3) + (half,)
    cos, sin = jnp.cos(ang).reshape(shape), jnp.sin(ang).reshape(shape)
    x1, x2 = x[..., :half], x[..., half:]
    return jnp.concatenate([x1 * cos - x2 * sin, x2 * cos + x1 * sin], axis=-1).astype(x.dtype)


def ssd_scan(xs, dt, a, bm, cm, s0):
    bt, l = xs.shape[:2]
    nc = -(-l // SSD_BLOCK)
    pad = nc * SSD_BLOCK - l

    def blocks(t):
        t = jnp.pad(t, ((0, 0), (0, pad)) + ((0, 0),) * (t.ndim - 2))
        return t.reshape((bt, nc, SSD_BLOCK) + t.shape[2:])

    xdt = blocks(xs * dt[..., None])
    adt = jnp.moveaxis(blocks(dt * a), 2, -1)
    bb, cc = blocks(bm), blocks(cm)
    acs = jnp.cumsum(adt, axis=-1)
    causal = jnp.tril(jnp.ones((SSD_BLOCK, SSD_BLOCK), dtype=bool))
    seg = acs[..., :, None] - acs[..., None, :]
    lmat = jnp.where(causal, jnp.exp(jnp.where(causal, seg, 0.0)), 0.0)
    cb = jnp.einsum('bclgn,bcsgn->bcgls', cc, bb)
    y_diag = jnp.einsum('bcgls,bcgrls,bcsgrp->bclgrp', cb, lmat, xdt)
    decay_to_end = jnp.exp(acs[..., -1:] - acs)
    chunk_states = jnp.einsum('bclgn,bcgrl,bclgrp->bcgrpn', bb, decay_to_end, xdt).astype(jnp.float32)
    chunk_decay = jnp.exp(acs[..., -1])

    def step(s, inp):
        cs, cd = inp
        return s * cd[..., None, None] + cs, s

    s_init = s0.astype(jnp.float32).reshape(bt, SSD_GROUPS, SSD_HPG, SSD_HEADDIM, SSD_STATE)
    s_final, s_prev = lax.scan(step, s_init, (jnp.moveaxis(chunk_states, 1, 0), jnp.moveaxis(chunk_decay, 1, 0)))
    s_prev = jnp.moveaxis(s_prev, 0, 1)
    y_off = jnp.einsum('bclgn,bcgrpn,bcgrl->bclgrp', cc, s_prev, jnp.exp(acs))
    y = (y_diag + y_off).reshape((bt, nc * SSD_BLOCK) + xs.shape[2:])[:, :l]
    return y, s_final.reshape(bt, SSD_HEADS, SSD_HEADDIM, SSD_STATE)


def mamba_mixer(x, conv_prev, ssm_prev, w_in, conv_w, conv_b, dt_bias, a_log, d_skip, norm_w, w_out):
    bt, l, _ = x.shape
    proj = x @ w_in
    z = proj[..., :SSD_INNER]
    xbc = proj[..., SSD_INNER:SSD_INNER + SSD_CONV_DIM]
    dt_raw = proj[..., SSD_INNER + SSD_CONV_DIM:]
    xbc, conv_new = causal_dwconv(xbc, conv_prev, conv_w, conv_b)
    xbc = jax.nn.silu(xbc)
    gn = SSD_GROUPS * SSD_STATE
    xs = xbc[..., :SSD_INNER].reshape(bt, l, SSD_GROUPS, SSD_HPG, SSD_HEADDIM)
    bm = xbc[..., SSD_INNER:SSD_INNER + gn].reshape(bt, l, SSD_GROUPS, SSD_STATE)
    cm = xbc[..., SSD_INNER + gn:].reshape(bt, l, SSD_GROUPS, SSD_STATE)
    dt = jax.nn.softplus((dt_raw + dt_bias).astype(jnp.float32)).reshape(bt, l, SSD_GROUPS, SSD_HPG)
    a = -jnp.exp(a_log.astype(jnp.float32)).reshape(SSD_GROUPS, SSD_HPG)
    y, ssm_new = ssd_scan(xs, dt, a, bm, cm, ssm_prev)
    y = y + d_skip.reshape(SSD_GROUPS, SSD_HPG)[..., None] * xs
    y = y.reshape(bt, l, SSD_INNER).astype(x.dtype)
    y = rms_norm(y * jax.nn.silu(z), norm_w, SSD_NORM_EPS)
    return y @ w_out, conv_new, ssm_new.astype(x.dtype)


def diff_attention(q, k, v, q_chunk, k_chunk, lam):
    bt, lq = q.shape[:2]
    qb = min(Q_BLOCK, lq)
    nb = -(-lq // qb)
    pad = nb * qb - lq
    qp = jnp.pad(q, ((0, 0), (0, pad), (0, 0), (0, 0), (0, 0)))
    qp = jnp.swapaxes(qp.reshape((bt, nb, qb) + q.shape[2:]), 0, 1)
    qcp = jnp.pad(q_chunk, (0, pad), mode='edge').reshape(nb, qb)
    scale = DA_HD ** -0.5

    def block(args):
        qblk, qc = args
        s = jnp.einsum('bqhcd,bkhcd->bhcqk', qblk, k).astype(jnp.float32) * scale
        visible = k_chunk[None, :] <= qc[:, None]
        p = jax.nn.softmax(jnp.where(visible, s, -jnp.inf), axis=-1)
        attn = p[:, :, 0] - lam * p[:, :, 1]
        return jnp.einsum('bhqk,bkhe->bqhe', attn.astype(v.dtype), v)

    out = lax.map(block, (qp, qcp))
    return jnp.swapaxes(out, 0, 1).reshape(bt, nb * qb, DA_HEADS, 2 * DA_HD)[:, :lq]


def diff_attn_mixer(x, pos, q_chunk, k_chunk, k_past, v_past, w_qkv, lam_p, subln_g, w_o, lam_init):
    bt, l, _ = x.shape
    q, k, v = jnp.split(x @ w_qkv, 3, axis=-1)
    q = rotary(q.reshape(bt, l, DA_HEADS, 2, DA_HD), pos)
    k = rotary(k.reshape(bt, l, DA_HEADS, 2, DA_HD), pos)
    v = v.reshape(bt, l, DA_HEADS, 2 * DA_HD)
    lp = lam_p.astype(jnp.float32)
    lam = jnp.exp(jnp.sum(lp[0] * lp[1])) - jnp.exp(jnp.sum(lp[2] * lp[3])) + lam_init
    k_all = jnp.concatenate([k_past.astype(k.dtype), k], axis=1)
    v_all = jnp.concatenate([v_past.astype(v.dtype), v], axis=1)
    o = diff_attention(q, k_all, v_all, q_chunk, k_chunk, lam)
    o = rms_norm(o, subln_g, SUBLN_EPS) * (1.0 - lam_init)
    return o.reshape(bt, l, D_MODEL) @ w_o, k, v


def wkv7_scan(r, w, k, v, a, b, s0):
    def step(s, inp):
        rt, wt, kt, vt, at, btt = inp
        sa = jnp.einsum('bhvk,bhk->bhv', s, at)
        s = s * wt[:, :, None, :] + sa[..., None] * btt[:, :, None, :] + vt[..., None] * kt[:, :, None, :]
        return s, jnp.einsum('bhvk,bhk->bhv', s, rt)

    seq = tuple(jnp.swapaxes(t.astype(jnp.float32), 0, 1) for t in (r, w, k, v, a, b))
    s_final, y = lax.scan(step, s0.astype(jnp.float32), seq)
    return jnp.swapaxes(y, 0, 1), s_final


def rwkv_mixer(x, shift_prev, wkv_prev, mu, w0, w1, w2, a0, a1, a2, g1, g2, k_k, k_a, r_k,
               w_r, w_k, w_v, w_o, lnx_g, lnx_b):
    bt, l, _ = x.shape
    x_prev = jnp.concatenate([shift_prev.astype(x.dtype), x[:, :-1]], axis=1)
    xx = x_prev - x
    xr, xw, xk, xv, xa, xg = (x + xx * mu[n] for n in range(6))
    r, k, v = xr @ w_r, xk @ w_k, xv @ w_v
    w_log = -jax.nn.softplus(-(w0 + jnp.tanh(xw @ w1) @ w2).astype(jnp.float32)) - 0.5
    decay = jnp.exp(-jnp.exp(w_log))
    a = jax.nn.sigmoid(a0 + (xa @ a1) @ a2)
    g = jax.nn.sigmoid(xg @ g1) @ g2

    def heads(t):
        return t.reshape(bt, l, RK_HEADS, RK_HD)

    kk = heads(k * k_k).astype(jnp.float32)
    kk = kk / jnp.maximum(jnp.sqrt(jnp.sum(jnp.square(kk), axis=-1, keepdims=True)), 1e-12)
    k = k * (1.0 + (a - 1.0) * k_a)
    r_h, k_h, v_h, a_h = heads(r), heads(k), heads(v), heads(a)
    y, wkv_new = wkv7_scan(r_h, heads(decay), k_h, v_h, -kk, kk * a_h, wkv_prev)
    mean = jnp.mean(y, axis=-1, keepdims=True)
    var = jnp.mean(jnp.square(y - mean), axis=-1, keepdims=True)
    y = ((y - mean) * lax.rsqrt(var + RK_GN_EPS)).reshape(bt, l, D_MODEL) * lnx_g + lnx_b
    bonus = jnp.sum(r_h * k_h * r_k, axis=-1, keepdims=True) * v_h
    y = (y + bonus.reshape(bt, l, D_MODEL)).astype(x.dtype)
    return (y * g) @ w_o, x[:, -1:], wkv_new.astype(x.dtype)


def conv_ffn(x, conv_prev, w_up, w_gate, conv_w, conv_b, w_down):
    gate, conv_new = causal_dwconv(x @ w_gate, conv_prev, conv_w, conv_b)
    return (jax.nn.silu(gate) * (x @ w_up)) @ w_down, conv_new


def trunk(x, pos, q_chunk, k_chunk, hist, params):
    new = {name: [] for name in STATE_NAMES}
    for i in range(DEPTH):
        j, kind = i // N_MIXERS, i % N_MIXERS
        if kind == 0:
            y, conv_new, ssm_new = mamba_mixer(x, hist['ssd_conv'][j], hist['ssm'][j],
                                               *[params[n][j] for n in SSD_NAMES])
            new['ssd_conv'].append(conv_new)
            new['ssm'].append(ssm_new)
        elif kind == 1:
            lam_init = 0.8 - 0.6 * math.exp(-0.3 * i)
            y, k_new, v_new = diff_attn_mixer(x, pos, q_chunk, k_chunk, hist['k'][j], hist['v'][j],
                                              *[params[n][j] for n in DA_NAMES], lam_init)
            new['k'].append(k_new)
            new['v'].append(v_new)
        else:
            y, shift_new, wkv_new = rwkv_mixer(x, hist['shift'][j], hist['wkv'][j],
                                               *[params[n][j] for n in RK_NAMES])
            new['shift'].append(shift_new)
            new['wkv'].append(wkv_new)
        x = layer_norm(DEEPNORM_ALPHA * x + y, params['ln_g'][i, 0], params['ln_b'][i, 0])
        f, fconv_new = conv_ffn(x, hist['ffn_conv'][i], *[params[n][i] for n in FFN_NAMES])
        new['ffn_conv'].append(fconv_new)
        x = layer_norm(DEEPNORM_ALPHA * x + f, params['ln_g'][i, 1], params['ln_b'][i, 1])
    return x, tuple(jnp.stack(new[n]) for n in STATE_NAMES)


def setup_inputs(seed: int = 0) -> dict:
    key = jax.random.key(seed)
    ks = iter(jax.random.split(key, 64))

    def nrm(shape, scale):
        return jax.random.normal(next(ks), shape, jnp.float32) * scale

    def uni(shape, lo, hi):
        return jax.random.uniform(next(ks), shape, jnp.float32, lo, hi)

    d = D_MODEL
    dt0 = jnp.exp(uni((N_SSD_LAYERS, SSD_HEADS), math.log(1e-3), math.log(1e-1)))
    return {
        'x_prompt': nrm((BATCH, SEQ, d), 1.0),
        'x_sample': nrm((DEC_BATCH, DEC_SEQ, d), 1.0),
        'cache_attn_k': nrm((N_DA_LAYERS, DEC_BATCH, PAST_LEN, DA_HEADS, 2, DA_HD), 1.0),
        'cache_attn_v': nrm((N_DA_LAYERS, DEC_BATCH, PAST_LEN, DA_HEADS, 2 * DA_HD), 1.0),
        'state_ssm': nrm((N_SSD_LAYERS, DEC_BATCH, SSD_HEADS, SSD_HEADDIM, SSD_STATE), 0.1),
        'state_ssd_conv': nrm((N_SSD_LAYERS, DEC_BATCH, SSD_CONV - 1, SSD_CONV_DIM), 1.0),
        'state_wkv': nrm((N_RK_LAYERS, DEC_BATCH, RK_HEADS, RK_HD, RK_HD), 0.1),
        'state_rwkv_shift': nrm((N_RK_LAYERS, DEC_BATCH, 1, d), 1.0),
        'state_ffn_conv': nrm((DEPTH, DEC_BATCH, FFN_CONV - 1, D_FF), 1.0),
        'meta_tokens': nrm((N_META, d), 1.0),
        'ssd_w_in': nrm((N_SSD_LAYERS, d, SSD_PROJ), d ** -0.5),
        'ssd_conv_w': nrm((N_SSD_LAYERS, SSD_CONV, SSD_CONV_DIM), SSD_CONV ** -0.5),
        'ssd_conv_b': nrm((N_SSD_LAYERS, SSD_CONV_DIM), 0.02),
        'ssd_dt_bias': dt0 + jnp.log(-jnp.expm1(-dt0)),
        'ssd_a_log': jnp.log(uni((N_SSD_LAYERS, SSD_HEADS), 1.0, 16.0)),
        'ssd_d': 1.0 + nrm((N_SSD_LAYERS, SSD_HEADS), 0.1),
        'ssd_norm_w': 1.0 + nrm((N_SSD_LAYERS, SSD_INNER), 0.02),
        'ssd_w_out': nrm((N_SSD_LAYERS, SSD_INNER, d), SSD_INNER ** -0.5 * DEEPNORM_BETA),
        'da_w_qkv': nrm((N_DA_LAYERS, d, 3 * d), d ** -0.5),
        'da_lambda': nrm((N_DA_LAYERS, 4, DA_HD), 0.1),
        'da_subln_g': 1.0 + nrm((N_DA_LAYERS, 2 * DA_HD), 0.02),
        'da_w_o': nrm((N_DA_LAYERS, d, d), d ** -0.5 * DEEPNORM_BETA),
        'rk_mu': uni((N_RK_LAYERS, 6, d), 0.0, 1.0),
        'rk_w0': uni((N_RK_LAYERS, d), -6.0, 0.0),
        'rk_w1': nrm((N_RK_LAYERS, d, RK_DECAY_LORA), d ** -0.5),
        'rk_w2': nrm((N_RK_LAYERS, RK_DECAY_LORA, d), 0.1 * RK_DECAY_LORA ** -0.5),
        'rk_a0': nrm((N_RK_LAYERS, d), 0.1),
        'rk_a1': nrm((N_RK_LAYERS, d, RK_AAA_LORA), d ** -0.5),
        'rk_a2': nrm((N_RK_LAYERS, RK_AAA_LORA, d), 0.1 * RK_AAA_LORA ** -0.5),
        'rk_g1': nrm((N_RK_LAYERS, d, RK_GATE_LORA), d ** -0.5),
        'rk_g2': nrm((N_RK_LAYERS, RK_GATE_LORA, d), RK_GATE_LORA ** -0.5),
        'rk_k_k': 0.85 + nrm((N_RK_LAYERS, d), 0.02),
        'rk_k_a': 1.0 + nrm((N_RK_LAYERS, d), 0.02),
        'rk_r_k': nrm((N_RK_LAYERS, RK_HEADS, RK_HD), 0.1),
        'rk_w_r': nrm((N_RK_LAYERS, d, d), d ** -0.5),
        'rk_w_k': nrm((N_RK_LAYERS, d, d), d ** -0.5),
        'rk_w_v': nrm((N_RK_LAYERS, d, d), d ** -0.5),
        'rk_w_o': nrm((N_RK_LAYERS, d, d), d ** -0.5 * DEEPNORM_BETA),
        'rk_lnx_g': 1.0 + nrm((N_RK_LAYERS, d), 0.02),
        'rk_lnx_b': nrm((N_RK_LAYERS, d), 0.02),
        'ffn_w_up': nrm((DEPTH, d, D_FF), d ** -0.5),
        'ffn_w_gate': nrm((DEPTH, d, D_FF), d ** -0.5),
        'ffn_conv_w': nrm((DEPTH, FFN_CONV, D_FF), FFN_CONV ** -0.5),
        'ffn_conv_b': nrm((DEPTH, D_FF), 0.02),
        'ffn_w_down': nrm((DEPTH, D_FF, d), D_FF ** -0.5 * DEEPNORM_BETA),
        'ln_g': 1.0 + nrm((DEPTH, 2, d), 0.02),
        'ln_b': nrm((DEPTH, 2, d), 0.02),
    }


def reference(x_prompt, x_sample, cache_attn_k, cache_attn_v, state_ssm, state_ssd_conv, state_wkv,
              state_rwkv_shift, state_ffn_conv, meta_tokens, ssd_w_in, ssd_conv_w, ssd_conv_b, ssd_dt_bias,
              ssd_a_log, ssd_d, ssd_norm_w, ssd_w_out, da_w_qkv, da_lambda, da_subln_g, da_w_o, rk_mu, rk_w0,
              rk_w1, rk_w2, rk_a0, rk_a1, rk_a2, rk_g1, rk_g2, rk_k_k, rk_k_a, rk_r_k, rk_w_r, rk_w_k, rk_w_v,
              rk_w_o, rk_lnx_g, rk_lnx_b, ffn_w_up, ffn_w_gate, ffn_conv_w, ffn_conv_b, ffn_w_down, ln_g, ln_b):
    params = dict(ssd_w_in=ssd_w_in, ssd_conv_w=ssd_conv_w, ssd_conv_b=ssd_conv_b, ssd_dt_bias=ssd_dt_bias,
                  ssd_a_log=ssd_a_log, ssd_d=ssd_d, ssd_norm_w=ssd_norm_w, ssd_w_out=ssd_w_out,
                  da_w_qkv=da_w_qkv, da_lambda=da_lambda, da_subln_g=da_subln_g, da_w_o=da_w_o,
                  rk_mu=rk_mu, rk_w0=rk_w0, rk_w1=rk_w1, rk_w2=rk_w2, rk_a0=rk_a0, rk_a1=rk_a1, rk_a2=rk_a2,
                  rk_g1=rk_g1, rk_g2=rk_g2, rk_k_k=rk_k_k, rk_k_a=rk_k_a, rk_r_k=rk_r_k, rk_w_r=rk_w_r,
                  rk_w_k=rk_w_k, rk_w_v=rk_w_v, rk_w_o=rk_w_o, rk_lnx_g=rk_lnx_g, rk_lnx_b=rk_lnx_b,
                  ffn_w_up=ffn_w_up, ffn_w_gate=ffn_w_gate, ffn_conv_w=ffn_conv_w, ffn_conv_b=ffn_conv_b,
                  ffn_w_down=ffn_w_down, ln_g=ln_g, ln_b=ln_b)

    bp, dtp = x_prompt.shape[0], x_prompt.dtype
    meta = jnp.broadcast_to(meta_tokens.astype(dtp)[None], (bp, N_META, D_MODEL))
    xp = jnp.concatenate([meta, x_prompt], axis=1)
    pos_p = jnp.arange(xp.shape[1])
    chunk_p = (pos_p - N_META) // CHUNK + 1
    hist_p = dict(
        k=jnp.zeros((N_DA_LAYERS, bp, 0, DA_HEADS, 2, DA_HD), dtp),
        v=jnp.zeros((N_DA_LAYERS, bp, 0, DA_HEADS, 2 * DA_HD), dtp),
        ssm=jnp.zeros((N_SSD_LAYERS, bp, SSD_HEADS, SSD_HEADDIM, SSD_STATE), dtp),
        ssd_conv=jnp.zeros((N_SSD_LAYERS, bp, SSD_CONV - 1, SSD_CONV_DIM), dtp),
        wkv=jnp.zeros((N_RK_LAYERS, bp, RK_HEADS, RK_HD, RK_HD), dtp),
        shift=jnp.zeros((N_RK_LAYERS, bp, 1, D_MODEL), dtp),
        ffn_conv=jnp.zeros((DEPTH, bp, FFN_CONV - 1, D_FF), dtp))
    yp, (k_p, v_p, ssm_p, sc_p, wkv_p, sh_p, fc_p) = trunk(xp, pos_p, chunk_p, chunk_p, hist_p, params)
    y_prompt = yp[:, N_META:]

    ls, past = x_sample.shape[1], cache_attn_k.shape[2]
    pos_s = past + jnp.arange(ls)
    hist_s = dict(k=cache_attn_k, v=cache_attn_v, ssm=state_ssm, ssd_conv=state_ssd_conv, wkv=state_wkv,
                  shift=state_rwkv_shift, ffn_conv=state_ffn_conv)
    y_sample, (k_s, v_s, ssm_s, sc_s, wkv_s, sh_s, fc_s) = trunk(
        x_sample, pos_s, pos_s // CHUNK, jnp.arange(past + ls) // CHUNK, hist_s, params)

    return (y_prompt, y_sample, k_p, v_p, ssm_p, sc_p, wkv_p, sh_p, fc_p,
            k_s, v_s, ssm_s, sc_s, wkv_s, sh_s, fc_s)
```

```python
import functools
import math

import jax
import jax.numpy as jnp
from jax import lax
from jax.experimental import pallas as pl
from jax.experimental.pallas import tpu as pltpu

F32 = jnp.float32
BF16 = jnp.bfloat16

CHUNK = 64
N_META = 16
N_MIXERS = 3
SSD_HEADDIM = 64
SSD_GROUPS = 8
SSD_STATE = 128
SSD_CONV = 4
SSD_NORM_EPS = 1e-5
DA_HD = 64
ROPE_THETA = 10000.0
SUBLN_EPS = 1e-5
RK_HD = 64
RK_GN_EPS = 64e-5
FFN_CONV = 3
LN_EPS = 1e-5

LANES = 128
SUBLANES = 8
BLK = 128
TM = 256
ATT_TQ = 256
ATT_TK = 256
VMEM_LIMIT = 56 * 1024 * 1024
NEG_BIG = -1e30

_NT = (((1,), (1,)), ((), ()))


def _cp():
    return pltpu.CompilerParams(vmem_limit_bytes=VMEM_LIMIT)


def _sigmoid(x):
    return 1.0 / (1.0 + jnp.exp(-x))


def _softplus(x):
    return jnp.maximum(x, 0.0) + jnp.log(1.0 + jnp.exp(-jnp.abs(x)))


def _bdot(a, b):
    return jnp.dot(a.astype(BF16), b.astype(BF16), preferred_element_type=F32)


def _bdot_nt(a, b):
    return lax.dot_general(a.astype(BF16), b.astype(BF16), _NT, preferred_element_type=F32)


def _mm_multi_body(*refs, n):
    x = refs[0][...].astype(BF16)
    for i in range(n):
        refs[1 + n + i][...] = jnp.dot(x, refs[1 + i][...], preferred_element_type=F32).astype(refs[1 + n + i].dtype)


def _mm_multi(x, ws, out_dtypes, name):
    m, k = x.shape
    n = len(ws)
    outs = pl.pallas_call(
        functools.partial(_mm_multi_body, n=n),
        grid=(m // TM,),
        in_specs=[pl.BlockSpec((TM, k), lambda i: (i, 0))]
        + [pl.BlockSpec(w.shape, lambda i: (0, 0)) for w in ws],
        out_specs=[pl.BlockSpec((TM, w.shape[1]), lambda i: (i, 0)) for w in ws],
        out_shape=[jax.ShapeDtypeStruct((m, w.shape[1]), dt) for w, dt in zip(ws, out_dtypes)],
        compiler_params=_cp(),
        name=name,
    )(x, *ws)
    return outs


def _mm_ln_body(x_ref, w_ref, r_ref, g_ref, b_ref, o_ref, *, alpha):
    acc = jnp.dot(x_ref[...].astype(BF16), w_ref[...], preferred_element_type=F32)
    h = alpha * r_ref[...] + acc
    mu = jnp.mean(h, axis=-1, keepdims=True)
    d = h - mu
    var = jnp.mean(d * d, axis=-1, keepdims=True)
    o_ref[...] = d * lax.rsqrt(var + LN_EPS) * g_ref[...] + b_ref[...]


def _mm_ln(x, w, resid, g, b, alpha, name):
    m, k = x.shape
    d = w.shape[1]
    return pl.pallas_call(
        functools.partial(_mm_ln_body, alpha=alpha),
        grid=(m // TM,),
        in_specs=[pl.BlockSpec((TM, k), lambda i: (i, 0)),
                  pl.BlockSpec((k, d), lambda i: (0, 0)),
                  pl.BlockSpec((TM, d), lambda i: (i, 0)),
                  pl.BlockSpec((1, d), lambda i: (0, 0)),
                  pl.BlockSpec((1, d), lambda i: (0, 0))],
        out_specs=pl.BlockSpec((TM, d), lambda i: (i, 0)),
        out_shape=jax.ShapeDtypeStruct((m, d), F32),
        compiler_params=_cp(),
        name=name,
    )(x, w, resid, g.reshape(1, d), b.reshape(1, d))


def _lora_body(x_ref, w1_ref, w2_ref, o_ref, *, act):
    h = jnp.dot(x_ref[...], w1_ref[...], preferred_element_type=F32)
    if act == 'tanh':
        h = jnp.tanh(h)
    elif act == 'sigmoid':
        h = _sigmoid(h)
    o_ref[...] = jnp.dot(h.astype(BF16), w2_ref[...], preferred_element_type=F32)


def _lora(x, w1, w2, act, name):
    m, k = x.shape
    r = w1.shape[1]
    rp = -(-r // LANES) * LANES
    w1p = jnp.pad(w1, ((0, 0), (0, rp - r))).astype(BF16)
    w2p = jnp.pad(w2, ((0, rp - r), (0, 0))).astype(BF16)
    d = w2.shape[1]
    return pl.pallas_call(
        functools.partial(_lora_body, act=act),
        grid=(m // TM,),
        in_specs=[pl.BlockSpec((TM, k), lambda i: (i, 0)),
                  pl.BlockSpec((k, rp), lambda i: (0, 0)),
                  pl.BlockSpec((rp, d), lambda i: (0, 0))],
        out_specs=pl.BlockSpec((TM, d), lambda i: (i, 0)),
        out_shape=jax.ShapeDtypeStruct((m, d), F32),
        compiler_params=_cp(),
        name=name,
    )(x, w1p, w2p)


def _block_info(b, nbp, lreal, dec_seq):
    is_start = jnp.logical_or(b == 0, b >= nbp)
    nvalid = jnp.where(b < nbp, jnp.clip(lreal - b * BLK, 0, BLK), dec_seq)
    return is_start, nvalid


def _stream_map(nbp, ndim):
    def index_map(b):
        return (jnp.maximum(b - nbp + 1, 0),) + (0,) * (ndim - 1)
    return index_map


def _expand_heads(colmat, g, lane_group):
    out = colmat[:, 4 * g + 3:4 * g + 4]
    for r in (2, 1, 0):
        out = jnp.where(lane_group == r, colmat[:, 4 * g + r:4 * g + r + 1], out)
    return out


def _ssd_body(xbc_ref, z_ref, dt_ref, tail_ref, s0_ref, cw_ref, cb_ref, dtb_ref, alog_ref, d_ref, nw_ref,
              y_ref, st_ref, win_ref, xc_ref, yb_ref, *, nbp, lreal, dec_seq):
    q = BLK
    inner = z_ref.shape[1]
    cdim = xbc_ref.shape[1]
    gn = SSD_GROUPS * SSD_STATE
    b = pl.program_id(0)
    is_start, nvalid = _block_info(b, nbp, lreal, dec_seq)

    @pl.when(is_start)
    def _():
        win_ref[0:SUBLANES, :] = tail_ref[0]
        st_ref[0] = s0_ref[0]

    win_ref[SUBLANES:SUBLANES + q, :] = xbc_ref[...]
    row = lax.broadcasted_iota(jnp.int32, (q, 1), 0)
    valid = row < nvalid
    cchunk = 512
    for c0 in range(0, cdim, cchunk):
        acc = jnp.broadcast_to(cb_ref[:, c0:c0 + cchunk], (q, cchunk))
        for j in range(SSD_CONV):
            lo = SUBLANES - (SSD_CONV - 1) + j
            acc = acc + win_ref[lo:lo + q, c0:c0 + cchunk] * cw_ref[j:j + 1, c0:c0 + cchunk]
        xc = acc * _sigmoid(acc)
        xc_ref[:, c0:c0 + cchunk] = jnp.where(valid, xc, 0.0)
    win_ref[0:SUBLANES, :] = win_ref[q:q + SUBLANES, :]

    dt = _softplus(dt_ref[...] + dtb_ref[...])
    dt = jnp.where(valid, dt, 0.0)
    a = -jnp.exp(alog_ref[...])
    adt = dt * a
    ri = lax.broadcasted_iota(jnp.int32, (q, q), 0)
    ci = lax.broadcasted_iota(jnp.int32, (q, q), 1)
    causal = ri >= ci
    tri = causal.astype(F32)
    acs = jnp.dot(tri, adt, preferred_element_type=F32, precision=lax.Precision.HIGHEST)
    acs_t = acs.T
    last = acs[q - 1:q, :]
    lane_group = lax.broadcasted_iota(jnp.int32, (1, 4 * SSD_HEADDIM), 1) // SSD_HEADDIM

    ssq = jnp.zeros((q, 1), F32)
    for g in range(SSD_GROUPS):
        bb = xc_ref[:, inner + g * SSD_STATE:inner + (g + 1) * SSD_STATE]
        cc = xc_ref[:, inner + gn + g * SSD_STATE:inner + gn + (g + 1) * SSD_STATE]
        xs = xc_ref[:, g * 256:(g + 1) * 256]
        cbm = _bdot_nt(cc, bb)
        dt_x = _expand_heads(dt, g, lane_group)
        acs_x = _expand_heads(acs, g, lane_group)
        last_x = _expand_heads(last, g, lane_group)
        xdt = xs * dt_x
        st = st_ref[0, g]
        y = _bdot(cc, st) * jnp.exp(acs_x)
        for r in range(4):
            h = 4 * g + r
            seg = acs[:, h:h + 1] - acs_t[h:h + 1, :]
            lmat = jnp.where(causal, jnp.exp(jnp.where(causal, seg, 0.0)), 0.0)
            att = cbm * lmat
            y = y + _bdot(att, jnp.where(lane_group == r, xdt, 0.0))
        xdtd = xdt * jnp.exp(last_x - acs_x)
        st_ref[0, g] = st * jnp.exp(last_x) + _bdot(bb.T, xdtd)
        y = y + _expand_heads(d_ref[...], g, lane_group) * xs
        zz = z_ref[:, g * 256:(g + 1) * 256]
        y = y * (zz * _sigmoid(zz))
        ssq = ssq + jnp.sum(y * y, axis=-1, keepdims=True)
        yb_ref[:, g * 256:(g + 1) * 256] = y
    scale = lax.rsqrt(ssq / inner + SSD_NORM_EPS)
    y_ref[...] = (yb_ref[...] * scale * nw_ref[...]).astype(y_ref.dtype)


def _ssd_scan(xbc, z, dt, tails, s0t, conv_w, conv_b, dt_bias, a_log, d_skip, norm_w, lay):
    m, cdim = xbc.shape
    inner = z.shape[1]
    nstream = tails.shape[0]
    nh = dt_bias.shape[0]

    def pad_h(v):
        return jnp.pad(v.astype(F32), (0, LANES - nh)).reshape(1, LANES)

    body = functools.partial(_ssd_body, nbp=lay['nbp'], lreal=lay['lreal'], dec_seq=lay['dec_seq'])
    smap3 = _stream_map(lay['nbp'], 3)
    smap4 = _stream_map(lay['nbp'], 4)
    return pl.pallas_call(
        body,
        grid=(m // BLK,),
        in_specs=[pl.BlockSpec((BLK, cdim), lambda b: (b, 0)),
                  pl.BlockSpec((BLK, inner), lambda b: (b, 0)),
                  pl.BlockSpec((BLK, LANES), lambda b: (b, 0)),
                  pl.BlockSpec((1, SUBLANES, cdim), smap3),
                  pl.BlockSpec((1,) + s0t.shape[1:], smap4),
                  pl.BlockSpec((SSD_CONV, cdim), lambda b: (0, 0)),
                  pl.BlockSpec((1, cdim), lambda b: (0, 0)),
                  pl.BlockSpec((1, LANES), lambda b: (0, 0)),
                  pl.BlockSpec((1, LANES), lambda b: (0, 0)),
                  pl.BlockSpec((1, LANES), lambda b: (0, 0)),
                  pl.BlockSpec((1, inner), lambda b: (0, 0))],
        out_specs=[pl.BlockSpec((BLK, inner), lambda b: (b, 0)),
                   pl.BlockSpec((1,) + s0t.shape[1:], smap4)],
        out_shape=[jax.ShapeDtypeStruct((m, inner), BF16),
                   jax.ShapeDtypeStruct(s0t.shape, F32)],
        scratch_shapes=[pltpu.VMEM((SUBLANES + BLK, cdim), F32),
                        pltpu.VMEM((BLK, cdim), F32),
                        pltpu.VMEM((BLK, inner), F32)],
        compiler_params=_cp(),
        name='ssd_scan',
    )(xbc, z, dt, tails, s0t, conv_w, conv_b.reshape(1, cdim), pad_h(dt_bias), pad_h(a_log), pad_h(d_skip),
      norm_w.reshape(1, inner))


def _ffn_gate_body(gate_ref, up_ref, tail_ref, cw_ref, cb_ref, h_ref, win_ref, *, nbp, lreal, dec_seq):
    q = BLK
    b = pl.program_id(0)
    is_start, _ = _block_info(b, nbp, lreal, dec_seq)

    @pl.when(is_start)
    def _():
        win_ref[0:SUBLANES, :] = tail_ref[0]

    win_ref[SUBLANES:SUBLANES + q, :] = gate_ref[...]
    acc = jnp.broadcast_to(cb_ref[...], gate_ref.shape)
    for j in range(FFN_CONV):
        lo = SUBLANES - (FFN_CONV - 1) + j
        acc = acc + win_ref[lo:lo + q, :] * cw_ref[j:j + 1, :]
    h_ref[...] = (acc * _sigmoid(acc) * up_ref[...]).astype(h_ref.dtype)
    win_ref[0:SUBLANES, :] = win_ref[q:q + SUBLANES, :]


def _ffn_gate(gate, up, tails, conv_w, conv_b, lay):
    m, f = gate.shape
    body = functools.partial(_ffn_gate_body, nbp=lay['nbp'], lreal=lay['lreal'], dec_seq=lay['dec_seq'])
    return pl.pallas_call(
        body,
        grid=(m // BLK,),
        in_specs=[pl.BlockSpec((BLK, f), lambda b: (b, 0)),
                  pl.BlockSpec((BLK, f), lambda b: (b, 0)),
                  pl.BlockSpec((1, SUBLANES, f), _stream_map(lay['nbp'], 3)),
                  pl.BlockSpec((FFN_CONV, f), lambda b: (0, 0)),
                  pl.BlockSpec((1, f), lambda b: (0, 0))],
        out_specs=pl.BlockSpec((BLK, f), lambda b: (b, 0)),
        out_shape=jax.ShapeDtypeStruct((m, f), BF16),
        scratch_shapes=[pltpu.VMEM((SUBLANES + BLK, f), F32)],
        compiler_params=_cp(),
        name='ffn_gate',
    )(gate, up, tails, conv_w, conv_b.reshape(1, f))


def _rope_body(qk_ref, v_ref, cos_ref, sin_ref, q16_ref, k32_ref, k16_ref, v16_ref, *, d_model, scale):
    cos = cos_ref[...]
    sin = sin_ref[...]
    lane = lax.broadcasted_iota(jnp.int32, cos.shape, 1)
    first_half = (lane % DA_HD) < (DA_HD // 2)
    for c in range(2 * d_model // LANES):
        x = qk_ref[:, c * LANES:(c + 1) * LANES]
        partner = jnp.where(first_half, -pltpu.roll(x, LANES - DA_HD // 2, 1), pltpu.roll(x, DA_HD // 2, 1))
        out = x * cos + partner * sin
        if c < d_model // LANES:
            q16_ref[:, c * LANES:(c + 1) * LANES] = (out * scale).astype(BF16)
        else:
            c2 = c - d_model // LANES
            k32_ref[:, c2 * LANES:(c2 + 1) * LANES] = out
            k16_ref[:, c2 * LANES:(c2 + 1) * LANES] = out.astype(BF16)
    v16_ref[...] = v_ref[...].astype(BF16)


def _rope(qkv, cos, sin, d_model):
    m = qkv.shape[0]
    return pl.pallas_call(
        functools.partial(_rope_body, d_model=d_model, scale=DA_HD ** -0.5),
        grid=(m // TM,),
        in_specs=[pl.BlockSpec((TM, 2 * d_model), lambda i: (i, 0)),
                  pl.BlockSpec((TM, d_model), lambda i: (i, 2)),
                  pl.BlockSpec((TM, LANES), lambda i: (i, 0)),
                  pl.BlockSpec((TM, LANES), lambda i: (i, 0))],
        out_specs=[pl.BlockSpec((TM, d_model), lambda i: (i, 0))] * 4,
        out_shape=[jax.ShapeDtypeStruct((m, d_model), BF16),
                   jax.ShapeDtypeStruct((m, d_model), F32),
                   jax.ShapeDtypeStruct((m, d_model), BF16),
                   jax.ShapeDtypeStruct((m, d_model), BF16)],
        compiler_params=_cp(),
        name='rope',
    )(qkv, qkv, cos, sin)


def _attn_tile(q0, q1, kt, vt, vis, carry, acc0_ref, acc1_ref):
    m0, l0, m1, l1 = carry
    new = []
    for qc, m_old, l_old, acc_ref in ((q0, m0, l0, acc0_ref), (q1, m1, l1, acc1_ref)):
        s = lax.dot_general(qc, kt, _NT, preferred_element_type=F32)
        s = jnp.where(vis, s, NEG_BIG)
        m_new = jnp.maximum(m_old, jnp.max(s, axis=-1, keepdims=True))
        p = jnp.exp(s - m_new)
        alpha = jnp.exp(m_old - m_new)
        l_new = alpha * l_old + jnp.sum(p, axis=-1, keepdims=True)
        acc_ref[...] = alpha * acc_ref[...] + jnp.dot(p.astype(BF16), vt, preferred_element_type=F32)
        new += [m_new, l_new]
    return tuple(new)


def _attn_finish(carry, acc0_ref, acc1_ref, lam_ref, g_ref, o_ref, lam_init):
    _, l0, _, l1 = carry
    lp = lam_ref[...]
    lam = (jnp.exp(jnp.sum(lp[0:1] * lp[1:2], axis=-1, keepdims=True))
           - jnp.exp(jnp.sum(lp[2:3] * lp[3:4], axis=-1, keepdims=True)) + lam_init)
    o = acc0_ref[...] / l0 - lam * (acc1_ref[...] / l1)
    ms = jnp.mean(o * o, axis=-1, keepdims=True)
    o_ref[...] = (o * lax.rsqrt(ms + SUBLN_EPS) * g_ref[...] * (1.0 - lam_init)).astype(o_ref.dtype)


def _attn_init(q_ref, acc0_ref, acc1_ref):
    q = q_ref[...]
    tq = q.shape[0]
    lane = lax.broadcasted_iota(jnp.int32, q.shape, 1)
    q0 = jnp.where(lane < DA_HD, q, jnp.zeros_like(q))
    q1 = jnp.where(lane >= DA_HD, q, jnp.zeros_like(q))
    acc0_ref[...] = jnp.zeros_like(acc0_ref)
    acc1_ref[...] = jnp.zeros_like(acc1_ref)
    carry = (jnp.full((tq, 1), NEG_BIG, F32), jnp.zeros((tq, 1), F32),
             jnp.full((tq, 1), NEG_BIG, F32), jnp.zeros((tq, 1), F32))
    return q0, q1, carry


def _attn_prompt_body(q_ref, k_ref, v_ref, lam_ref, g_ref, o_ref, acc0_ref, acc1_ref, *, tq, tk, lp, lreal, off,
                      lam_init):
    qi = pl.program_id(1)
    q0, q1, carry = _attn_init(q_ref, acc0_ref, acc1_ref)
    qpos = qi * tq + lax.broadcasted_iota(jnp.int32, (tq, 1), 0)
    qch = (qpos + off) >> 6
    last_chunk = (qi * tq + tq - 1 + off) >> 6
    kmax = last_chunk * CHUNK - off + CHUNK - 1
    nk = jnp.minimum(kmax // tk + 1, lp // tk)

    def body(ki, c):
        ks = pl.multiple_of(ki * tk, tk)
        kt = k_ref[pl.ds(ks, tk), :]
        vt = v_ref[pl.ds(ks, tk), :]
        kpos = ks + lax.broadcasted_iota(jnp.int32, (1, tk), 1)
        vis = jnp.logical_and(((kpos + off) >> 6) <= qch, kpos < lreal)
        return _attn_tile(q0, q1, kt, vt, vis, c, acc0_ref, acc1_ref)

    carry = lax.fori_loop(0, nk, body, carry)
    _attn_finish(carry, acc0_ref, acc1_ref, lam_ref, g_ref, o_ref, lam_init)


def _attn_sample_body(q_ref, kn_ref, vn_ref, kc_ref, vc_ref, lam_ref, g_ref, o_ref, acc0_ref, acc1_ref, *,
                      tk, past, dec_seq, lam_init):
    q0, q1, carry = _attn_init(q_ref, acc0_ref, acc1_ref)
    tq = q_ref.shape[0]
    qpos = past + lax.broadcasted_iota(jnp.int32, (tq, 1), 0)
    qch = qpos >> 6

    def body(ki, c):
        ks = pl.multiple_of(ki * tk, tk)
        kt = kc_ref[0, pl.ds(ks, tk), :].astype(BF16)
        vt = vc_ref[0, pl.ds(ks, tk), :].astype(BF16)
        kpos = ks + lax.broadcasted_iota(jnp.int32, (1, tk), 1)
        vis = (kpos >> 6) <= qch
        return _attn_tile(q0, q1, kt, vt, vis, c, acc0_ref, acc1_ref)

    carry = lax.fori_loop(0, past // tk, body, carry)
    kidx = lax.broadcasted_iota(jnp.int32, (1, tq), 1)
    vis = jnp.logical_and(((past + kidx) >> 6) <= qch, kidx < dec_seq)
    carry = _attn_tile(q0, q1, kn_ref[...], vn_ref[...], vis, carry, acc0_ref, acc1_ref)
    _attn_finish(carry, acc0_ref, acc1_ref, lam_ref, g_ref, o_ref, lam_init)


def _attention(q16, k16, v16, k_cache, v_cache, lam_p, subln_g, lam_init, lay):
    m, d = q16.shape
    nh = d // LANES
    lp = lay['lp']
    ns = lay['ns']
    past = k_cache.shape[1]
    g = subln_g.reshape(1, LANES).astype(F32)
    lam_p = lam_p.astype(F32)
    scratch = lambda tq: [pltpu.VMEM((tq, LANES), F32), pltpu.VMEM((tq, LANES), F32)]
    o_prompt = pl.pallas_call(
        functools.partial(_attn_prompt_body, tq=ATT_TQ, tk=ATT_TK, lp=lp, lreal=lay['lreal'], off=CHUNK - N_META,
                          lam_init=lam_init),
        grid=(nh, lp // ATT_TQ),
        in_specs=[pl.BlockSpec((ATT_TQ, LANES), lambda h, i: (i, h)),
                  pl.BlockSpec((lp, LANES), lambda h, i: (0, h)),
                  pl.BlockSpec((lp, LANES), lambda h, i: (0, h)),
                  pl.BlockSpec((4, DA_HD), lambda h, i: (0, 0)),
                  pl.BlockSpec((1, LANES), lambda h, i: (0, 0))],
        out_specs=pl.BlockSpec((ATT_TQ, LANES), lambda h, i: (i, h)),
        out_shape=jax.ShapeDtypeStruct((lp, d), BF16),
        scratch_shapes=scratch(ATT_TQ),
        compiler_params=_cp(),
        name='attn_prompt',
    )(q16, k16, v16, lam_p, g)
    nbp = lay['nbp']
    tk = min(512, past)
    o_sample = pl.pallas_call(
        functools.partial(_attn_sample_body, tk=tk, past=past, dec_seq=lay['dec_seq'], lam_init=lam_init),
        grid=(ns, nh),
        in_specs=[pl.BlockSpec((BLK, LANES), lambda s, h: (nbp + s, h)),
                  pl.BlockSpec((BLK, LANES), lambda s, h: (nbp + s, h)),
                  pl.BlockSpec((BLK, LANES), lambda s, h: (nbp + s, h)),
                  pl.BlockSpec((1, past, LANES), lambda s, h: (s, 0, h)),
                  pl.BlockSpec((1, past, LANES), lambda s, h: (s, 0, h)),
                  pl.BlockSpec((4, DA_HD), lambda s, h: (0, 0)),
                  pl.BlockSpec((1, LANES), lambda s, h: (0, 0))],
        out_specs=pl.BlockSpec((BLK, LANES), lambda s, h: (s, h)),
        out_shape=jax.ShapeDtypeStruct((ns * BLK, d), BF16),
        scratch_shapes=scratch(BLK),
        compiler_params=_cp(),
        name='attn_sample',
    )(q16, k16, v16, k_cache, v_cache, lam_p, g)
    return jnp.concatenate([o_prompt, o_sample], axis=0)


def _shift_mix_body(x_ref, tail_ref, mu_ref, *rest, nbp, lreal, dec_seq):
    outs, win_ref = rest[:-1], rest[-1]
    q = BLK
    b = pl.program_id(0)
    is_start, _ = _block_info(b, nbp, lreal, dec_seq)

    @pl.when(is_start)
    def _():
        win_ref[0:SUBLANES, :] = tail_ref[0]

    x = x_ref[...]
    win_ref[SUBLANES:SUBLANES + q, :] = x
    xx = win_ref[SUBLANES - 1:SUBLANES - 1 + q, :] - x
    for n, o_ref in enumerate(outs):
        o_ref[...] = (x + xx * mu_ref[n:n + 1, :]).astype(o_ref.dtype)
    win_ref[0:SUBLANES, :] = win_ref[q:q + SUBLANES, :]


def _shift_mix(x, tails, mu, lay):
    m, d = x.shape
    n = mu.shape[0]
    body = functools.partial(_shift_mix_body, nbp=lay['nbp'], lreal=lay['lreal'], dec_seq=lay['dec_seq'])
    return pl.pallas_call(
        body,
        grid=(m // BLK,),
        in_specs=[pl.BlockSpec((BLK, d), lambda b: (b, 0)),
                  pl.BlockSpec((1, SUBLANES, d), _stream_map(lay['nbp'], 3)),
                  pl.BlockSpec((n, d), lambda b: (0, 0))],
        out_specs=[pl.BlockSpec((BLK, d), lambda b: (b, 0))] * n,
        out_shape=[jax.ShapeDtypeStruct((m, d), BF16)] * n,
        scratch_shapes=[pltpu.VMEM((SUBLANES + BLK, d), F32)],
        compiler_params=_cp(),
        name='shift_mix',
    )(x, tails, mu)


def _half_sum(x, low):
    s_lo = jnp.sum(jnp.where(low, x, 0.0), axis=-1, keepdims=True)
    s_hi = jnp.sum(jnp.where(low, 0.0, x), axis=-1, keepdims=True)
    return jnp.where(low, s_lo, s_hi)


def _wkv_body(r_ref, k_ref, v_ref, lw_ref, la_ref, g_ref, s0_ref,
              w0_ref, a0_ref, kk_ref, ka_ref, lng_ref, lnb_ref, rk_ref,
              o_ref, st_ref, stage_ref, yb_ref, *, nbp, lreal, dec_seq):
    c = BLK
    npair = st_ref.shape[1]
    half = c // 2
    b = pl.program_id(0)
    is_start, nvalid = _block_info(b, nbp, lreal, dec_seq)

    @pl.when(is_start)
    def _():
        st_ref[0] = s0_ref[0]

    for i, ref in enumerate((r_ref, k_ref, v_ref, lw_ref, la_ref, g_ref)):
        for p in range(npair):
            stage_ref[i, p] = ref[:, p * LANES:(p + 1) * LANES]

    row = lax.broadcasted_iota(jnp.int32, (c, 1), 0)
    valid = row < nvalid
    lane = lax.broadcasted_iota(jnp.int32, (c, LANES), 1)
    low = lane < RK_HD
    ri = lax.broadcasted_iota(jnp.int32, (c, c), 0)
    ci = lax.broadcasted_iota(jnp.int32, (c, c), 1)
    lower_incl = ri >= ci
    lower_strict = ri > ci
    tri = lower_incl.astype(F32)
    eye = (ri == ci).astype(F32)
    same_head = (ri < RK_HD) == (ci < RK_HD)

    def pair(p, carry):
        r = stage_ref[0, p]
        k = stage_ref[1, p]
        v = jnp.where(valid, stage_ref[2, p], 0.0)
        lw = stage_ref[3, p]
        la = stage_ref[4, p]
        gate = stage_ref[5, p]
        w_log = -_softplus(-(w0_ref[p] + lw)) - 0.5
        logw = jnp.where(valid, -jnp.exp(w_log), 0.0)
        ag = _sigmoid(a0_ref[p] + la)
        kk = k * kk_ref[p]
        kk = kk / jnp.maximum(jnp.sqrt(_half_sum(kk * kk, low)), 1e-12)
        kx = k * (1.0 + (ag - 1.0) * ka_ref[p])
        a = jnp.where(valid, -kk, 0.0)
        bv = jnp.where(valid, kk * ag, 0.0)
        kxm = jnp.where(valid, kx, 0.0)

        cum = jnp.dot(tri, logw, preferred_element_type=F32, precision=lax.Precision.HIGHEST)
        cprev = cum - logw
        cmid = cum[half - 1:half, :]
        cend = cum[c - 1:c, :]
        a_abs = a * jnp.exp(cprev)
        r_abs = r * jnp.exp(cum)
        a_rel = a * jnp.exp(cprev - cmid)
        r_rel = r * jnp.exp(cum - cmid)
        e_inv = jnp.exp(cmid - cum)
        b_rel = bv * e_inv
        k_rel = kxm * e_inv
        e_end = jnp.exp(cend - cum)
        bk_end = jnp.concatenate([bv * e_end, kxm * e_end], axis=0)

        s = st_ref[0, p]
        es = _bdot_nt(jnp.concatenate([a_abs, r_abs], axis=0), s)
        a_s, r_s = es[:c], es[c:]
        bk_rel = jnp.concatenate([b_rel, k_rel], axis=0).astype(BF16)
        v16 = v.astype(BF16)

        def head(sel):
            ar = jnp.concatenate([jnp.where(sel, a_rel, 0.0), jnp.where(sel, r_rel, 0.0)], axis=0)
            mm = lax.dot_general(ar.astype(BF16), bk_rel, _NT, preferred_element_type=F32)
            n = jnp.where(lower_strict, mm[:c, :c], 0.0)
            x = eye + n
            pw = n
            for _ in range(int(math.log2(c)) - 1):
                pw = _bdot(pw, pw)
                x = x + _bdot(x, pw)
            rhs = a_s + _bdot(jnp.where(lower_strict, mm[:c, c:], 0.0), v16)
            u = _bdot(x, rhs)
            dd = jnp.concatenate([jnp.where(lower_incl, mm[c:, :c], 0.0),
                                  jnp.where(lower_incl, mm[c:, c:], 0.0)], axis=1)
            y = r_s + _bdot(dd, jnp.concatenate([u, v], axis=0))
            return u, y

        u_lo, y_lo = head(low)
        u_hi, y_hi = head(jnp.logical_not(low))
        u = jnp.where(low, u_lo, u_hi)
        y = jnp.where(low, y_lo, y_hi)
        uv_t = jnp.concatenate([u, v], axis=0).T
        s_new = s * jnp.exp(cend) + jnp.where(same_head, _bdot(uv_t, bk_end), 0.0)
        st_ref[0, p] = s_new

        mean = _half_sum(y, low) * (1.0 / RK_HD)
        dy = y - mean
        var = _half_sum(dy * dy, low) * (1.0 / RK_HD)
        yn = dy * lax.rsqrt(var + RK_GN_EPS) * lng_ref[p] + lnb_ref[p]
        bonus = _half_sum(r * kx * rk_ref[p], low) * stage_ref[2, p]
        yb_ref[p] = ((yn + bonus) * gate).astype(yb_ref.dtype)
        return carry

    lax.fori_loop(0, npair, pair, 0)
    for p in range(npair):
        o_ref[:, p * LANES:(p + 1) * LANES] = yb_ref[p]


def _wkv(r, k, v, lw, la, gate, s0, w0, a0, k_k, k_a, lnx_g, lnx_b, r_k, lay):
    m, d = r.shape
    npair = d // LANES

    def prm(t):
        return t.astype(F32).reshape(npair, 1, LANES)

    body = functools.partial(_wkv_body, nbp=lay['nbp'], lreal=lay['lreal'], dec_seq=lay['dec_seq'])
    row_spec = pl.BlockSpec((BLK, d), lambda b: (b, 0))
    prm_spec = pl.BlockSpec((npair, 1, LANES), lambda b: (0, 0, 0))
    smap = _stream_map(lay['nbp'], 4)
    return pl.pallas_call(
        body,
        grid=(m // BLK,),
        in_specs=[row_spec] * 6 + [pl.BlockSpec((1,) + s0.shape[1:], smap)] + [prm_spec] * 7,
        out_specs=[row_spec, pl.BlockSpec((1,) + s0.shape[1:], smap)],
        out_shape=[jax.ShapeDtypeStruct((m, d), BF16), jax.ShapeDtypeStruct(s0.shape, F32)],
        scratch_shapes=[pltpu.VMEM((6, npair, BLK, LANES), F32), pltpu.VMEM((npair, BLK, LANES), BF16)],
        compiler_params=_cp(),
        name='wkv7',
    )(r, k, v, lw, la, gate, s0, prm(w0), prm(a0), prm(k_k), prm(k_a), prm(lnx_g), prm(lnx_b), prm(r_k))


def _tails(prev, nstream):
    ns, w, c = prev.shape
    t = jnp.zeros((nstream, SUBLANES, c), F32)
    return t.at[1:, SUBLANES - w:, :].set(prev.astype(F32))


def _last_rows(flat, w, lay):
    lreal, lp, ns, dec = lay['lreal'], lay['lp'], lay['ns'], lay['dec_seq']
    p = flat[lreal - w:lreal][None]
    s = flat[lp:].reshape(ns, BLK, -1)[:, dec - w:dec]
    return p, s


def kernel(x_prompt, x_sample, cache_attn_k, cache_attn_v, state_ssm, state_ssd_conv, state_wkv, state_rwkv_shift, state_ffn_conv, meta_tokens, ssd_w_in, ssd_conv_w, ssd_conv_b, ssd_dt_bias, ssd_a_log, ssd_d, ssd_norm_w, ssd_w_out, da_w_qkv, da_lambda, da_subln_g, da_w_o, rk_mu, rk_w0, rk_w1, rk_w2, rk_a0, rk_a1, rk_a2, rk_g1, rk_g2, rk_k_k, rk_k_a, rk_r_k, rk_w_r, rk_w_k, rk_w_v, rk_w_o, rk_lnx_g, rk_lnx_b, ffn_w_up, ffn_w_gate, ffn_conv_w, ffn_conv_b, ffn_w_down, ln_g, ln_b):
    bp, seq, d = x_prompt.shape
    ns, dec_seq, _ = x_sample.shape
    depth = ln_g.shape[0]
    past = cache_attn_k.shape[2]
    assert bp == 1 and dec_seq >= SSD_CONV - 1 and dec_seq <= BLK and past % ATT_TK == 0
    lreal = N_META + seq
    lp = -(-lreal // TM) * TM
    m = lp + ns * BLK
    assert m % TM == 0 and lp % ATT_TQ == 0 and lp % ATT_TK == 0
    nstream = ns + 1
    lay = dict(lreal=lreal, lp=lp, nbp=lp // BLK, ns=ns, dec_seq=dec_seq)
    alpha = (2 * depth) ** 0.25

    x = jnp.concatenate([
        meta_tokens.astype(F32), x_prompt[0], jnp.zeros((lp - lreal, d), F32),
        jnp.pad(x_sample, ((0, 0), (0, BLK - dec_seq), (0, 0))).reshape(ns * BLK, d)], axis=0)

    pos = jnp.concatenate([jnp.arange(lp), jnp.tile(past + jnp.arange(BLK), ns)]).astype(F32)
    half = DA_HD // 2
    inv = ROPE_THETA ** (-jnp.arange(half, dtype=F32) / half)
    ang = pos[:, None] * inv[None, :]
    cos_t = jnp.tile(jnp.cos(ang), (1, LANES // half))
    sin_t = jnp.tile(jnp.sin(ang), (1, LANES // half))

    inner = ssd_norm_w.shape[1]
    nheads = ssd_dt_bias.shape[1]
    hpg = nheads // SSD_GROUPS
    new = {n: [] for n in ('k', 'v', 'ssm', 'ssd_conv', 'wkv', 'shift', 'ffn_conv')}

    for i in range(depth):
        j, kind = i // N_MIXERS, i % N_MIXERS
        if kind == 0:
            w_in = ssd_w_in[j]
            cdim = ssd_conv_w.shape[2]
            w_z = w_in[:, :inner].astype(BF16)
            w_xbc = w_in[:, inner:inner + cdim].astype(BF16)
            w_dt = jnp.pad(w_in[:, inner + cdim:], ((0, 0), (0, LANES - nheads))).astype(BF16)
            z, dt = _mm_multi(x, [w_z, w_dt], [F32, F32], 'ssd_in_z_dt')
            (xbc,) = _mm_multi(x, [w_xbc], [F32], 'ssd_in_xbc')
            s0 = jnp.concatenate([jnp.zeros((1,) + state_ssm.shape[2:], F32), state_ssm[j].astype(F32)], axis=0)
            s0t = s0.reshape(nstream, SSD_GROUPS, hpg * SSD_HEADDIM, SSD_STATE).swapaxes(2, 3)
            y, st = _ssd_scan(xbc, z, dt, _tails(state_ssd_conv[j], nstream), s0t, ssd_conv_w[j], ssd_conv_b[j],
                              ssd_dt_bias[j], ssd_a_log[j], ssd_d[j], ssd_norm_w[j], lay)
            w_o = ssd_w_out[j].astype(BF16)
            new['ssd_conv'].append(_last_rows(xbc, SSD_CONV - 1, lay))
            ssm = st.swapaxes(2, 3).reshape(nstream, nheads, SSD_HEADDIM, SSD_STATE)
            new['ssm'].append((ssm[:1], ssm[1:]))
        elif kind == 1:
            lam_init = 0.8 - 0.6 * math.exp(-0.3 * i)
            (qkv,) = _mm_multi(x, [da_w_qkv[j].astype(BF16)], [F32], 'da_qkv')
            q16, k32, k16, v16 = _rope(qkv, cos_t, sin_t, d)
            nh = d // (2 * DA_HD)
            y = _attention(q16, k16, v16, cache_attn_k[j].reshape(ns, past, d), cache_attn_v[j].reshape(ns, past, d),
                           da_lambda[j], da_subln_g[j], lam_init, lay)
            w_o = da_w_o[j].astype(BF16)
            kp, ks = k32[:lreal][None], k32[lp:].reshape(ns, BLK, d)[:, :dec_seq]
            v32 = qkv[:, 2 * d:]
            vp, vs = v32[:lreal][None], v32[lp:].reshape(ns, BLK, d)[:, :dec_seq]
            new['k'].append((kp.reshape(1, lreal, nh, 2, DA_HD), ks.reshape(ns, dec_seq, nh, 2, DA_HD)))
            new['v'].append((vp.reshape(1, lreal, nh, 2 * DA_HD), vs.reshape(ns, dec_seq, nh, 2 * DA_HD)))
        else:
            xr, xw, xk, xv, xa, xg = _shift_mix(x, _tails(state_rwkv_shift[j], nstream), rk_mu[j], lay)
            (r,) = _mm_multi(xr, [rk_w_r[j].astype(BF16)], [F32], 'rk_r')
            (k,) = _mm_multi(xk, [rk_w_k[j].astype(BF16)], [F32], 'rk_k')
            (v,) = _mm_multi(xv, [rk_w_v[j].astype(BF16)], [F32], 'rk_v')
            lw = _lora(xw, rk_w1[j], rk_w2[j], 'tanh', 'rk_lora_w')
            la = _lora(xa, rk_a1[j], rk_a2[j], 'none', 'rk_lora_a')
            gate = _lora(xg, rk_g1[j], rk_g2[j], 'sigmoid', 'rk_lora_g')
            nrh = d // RK_HD
            npair = nrh // 2
            sw = jnp.concatenate([jnp.zeros((1,) + state_wkv.shape[2:], F32), state_wkv[j].astype(F32)], axis=0)
            sw = sw.reshape(nstream, npair, 2, RK_HD, RK_HD)
            zero = jnp.zeros_like(sw[:, :, 0])
            s0 = jnp.concatenate([jnp.concatenate([sw[:, :, 0], zero], axis=-1),
                                  jnp.concatenate([zero, sw[:, :, 1]], axis=-1)], axis=-2)
            y, st = _wkv(r, k, v, lw, la, gate, s0, rk_w0[j], rk_a0[j], rk_k_k[j], rk_k_a[j], rk_lnx_g[j],
                         rk_lnx_b[j], rk_r_k[j].reshape(-1), lay)
            w_o = rk_w_o[j].astype(BF16)
            wkv = jnp.stack([st[:, :, :RK_HD, :RK_HD], st[:, :, RK_HD:, RK_HD:]], axis=2).reshape(
                nstream, nrh, RK_HD, RK_HD)
            new['wkv'].append((wkv[:1], wkv[1:]))
            new['shift'].append(_last_rows(x, 1, lay))
        x = _mm_ln(y, w_o, x, ln_g[i, 0], ln_b[i, 0], alpha, 'mixer_out_ln')
        gate_pre, up = _mm_multi(x, [ffn_w_gate[i].astype(BF16), ffn_w_up[i].astype(BF16)], [F32, F32], 'ffn_in')
        h = _ffn_gate(gate_pre, up, _tails(state_ffn_conv[i], nstream), ffn_conv_w[i], ffn_conv_b[i], lay)
        new['ffn_conv'].append(_last_rows(gate_pre, FFN_CONV - 1, lay))
        x = _mm_ln(h, ffn_w_down[i].astype(BF16), x, ln_g[i, 1], ln_b[i, 1], alpha, 'ffn_out_ln')

    y_prompt = x[N_META:lreal][None]
    y_sample = x[lp:].reshape(ns, BLK, d)[:, :dec_seq]
    order = ('k', 'v', 'ssm', 'ssd_conv', 'wkv', 'shift', 'ffn_conv')
    prompt_states = tuple(jnp.stack([t[0] for t in new[n]]) for n in order)
    sample_states = tuple(jnp.stack([t[1] for t in new[n]]) for n in order)
    return (y_prompt, y_sample) + prompt_states + sample_states
```

```python
import functools
import math

import jax
import jax.numpy as jnp
from jax import lax
from jax.experimental import pallas as pl
from jax.experimental.pallas import tpu as pltpu

F32 = jnp.float32
BF16 = jnp.bfloat16

CHUNK = 64
N_META = 16
N_MIXERS = 3
SSD_HEADDIM = 64
SSD_GROUPS = 8
SSD_STATE = 128
SSD_CONV = 4
SSD_NORM_EPS = 1e-5
DA_HD = 64
ROPE_THETA = 10000.0
SUBLN_EPS = 1e-5
RK_HD = 64
RK_GN_EPS = 64e-5
FFN_CONV = 3
LN_EPS = 1e-5

LANES = 128
SUBLANES = 8
BLK = 128
TM = 256
ATT_TQ = 512
ATT_TQ_SUB = 256
ATT_TK = 512
VMEM_LIMIT = 56 * 1024 * 1024
NEG_BIG = -1e30

_NT = (((1,), (1,)), ((), ()))


def _cp():
    return pltpu.CompilerParams(vmem_limit_bytes=VMEM_LIMIT)


def _sigmoid(x):
    return 1.0 / (1.0 + jnp.exp(-x))


def _softplus(x):
    return jnp.maximum(x, 0.0) + jnp.log(1.0 + jnp.exp(-jnp.abs(x)))


def _bdot(a, b):
    return jnp.dot(a.astype(BF16), b.astype(BF16), preferred_element_type=F32)


def _bdot_nt(a, b):
    return lax.dot_general(a.astype(BF16), b.astype(BF16), _NT, preferred_element_type=F32)


def _mm_multi_body(*refs, n):
    x = refs[0][...].astype(BF16)
    for i in range(n):
        refs[1 + n + i][...] = jnp.dot(x, refs[1 + i][...], preferred_element_type=F32).astype(refs[1 + n + i].dtype)


def _mm_multi(x, ws, out_dtypes, name):
    m, k = x.shape
    n = len(ws)
    outs = pl.pallas_call(
        functools.partial(_mm_multi_body, n=n),
        grid=(m // TM,),
        in_specs=[pl.BlockSpec((TM, k), lambda i: (i, 0))]
        + [pl.BlockSpec(w.shape, lambda i: (0, 0)) for w in ws],
        out_specs=[pl.BlockSpec((TM, w.shape[1]), lambda i: (i, 0)) for w in ws],
        out_shape=[jax.ShapeDtypeStruct((m, w.shape[1]), dt) for w, dt in zip(ws, out_dtypes)],
        compiler_params=_cp(),
        name=name,
    )(x, *ws)
    return outs


def _mm_ln_body(x_ref, w_ref, r_ref, g_ref, b_ref, o_ref, *, alpha):
    acc = jnp.dot(x_ref[...].astype(BF16), w_ref[...], preferred_element_type=F32)
    h = alpha * r_ref[...] + acc
    mu = jnp.mean(h, axis=-1, keepdims=True)
    d = h - mu
    var = jnp.mean(d * d, axis=-1, keepdims=True)
    o_ref[...] = d * lax.rsqrt(var + LN_EPS) * g_ref[...] + b_ref[...]


def _mm_ln(x, w, resid, g, b, alpha, name):
    m, k = x.shape
    d = w.shape[1]
    return pl.pallas_call(
        functools.partial(_mm_ln_body, alpha=alpha),
        grid=(m // TM,),
        in_specs=[pl.BlockSpec((TM, k), lambda i: (i, 0)),
                  pl.BlockSpec((k, d), lambda i: (0, 0)),
                  pl.BlockSpec((TM, d), lambda i: (i, 0)),
                  pl.BlockSpec((1, d), lambda i: (0, 0)),
                  pl.BlockSpec((1, d), lambda i: (0, 0))],
        out_specs=pl.BlockSpec((TM, d), lambda i: (i, 0)),
        out_shape=jax.ShapeDtypeStruct((m, d), F32),
        compiler_params=_cp(),
        name=name,
    )(x, w, resid, g.reshape(1, d), b.reshape(1, d))


def _lora_body(x_ref, w1_ref, w2_ref, o_ref, *, act):
    h = jnp.dot(x_ref[...], w1_ref[...], preferred_element_type=F32)
    if act == 'tanh':
        h = jnp.tanh(h)
    elif act == 'sigmoid':
        h = _sigmoid(h)
    o_ref[...] = jnp.dot(h.astype(BF16), w2_ref[...], preferred_element_type=F32)


def _lora(x, w1, w2, act, name):
    m, k = x.shape
    r = w1.shape[1]
    rp = -(-r // LANES) * LANES
    w1p = jnp.pad(w1, ((0, 0), (0, rp - r))).astype(BF16)
    w2p = jnp.pad(w2, ((0, rp - r), (0, 0))).astype(BF16)
    d = w2.shape[1]
    return pl.pallas_call(
        functools.partial(_lora_body, act=act),
        grid=(m // TM,),
        in_specs=[pl.BlockSpec((TM, k), lambda i: (i, 0)),
                  pl.BlockSpec((k, rp), lambda i: (0, 0)),
                  pl.BlockSpec((rp, d), lambda i: (0, 0))],
        out_specs=pl.BlockSpec((TM, d), lambda i: (i, 0)),
        out_shape=jax.ShapeDtypeStruct((m, d), F32),
        compiler_params=_cp(),
        name=name,
    )(x, w1p, w2p)


def _block_info(b, nbp, lreal, dec_seq):
    is_start = jnp.logical_or(b == 0, b >= nbp)
    nvalid = jnp.where(b < nbp, jnp.clip(lreal - b * BLK, 0, BLK), dec_seq)
    return is_start, nvalid


def _stream_map(nbp, ndim):
    def index_map(b):
        return (jnp.maximum(b - nbp + 1, 0),) + (0,) * (ndim - 1)
    return index_map


def _expand_heads(colmat, g, lane_group):
    out = colmat[:, 4 * g + 3:4 * g + 4]
    for r in (2, 1, 0):
        out = jnp.where(lane_group == r, colmat[:, 4 * g + r:4 * g + r + 1], out)
    return out


def _ssd_body(xbc_ref, z_ref, dt_ref, tail_ref, s0_ref, cw_ref, cb_ref, dtb_ref, alog_ref, d_ref, nw_ref,
              y_ref, st_ref, win_ref, xc_ref, yb_ref, *, nbp, lreal, dec_seq):
    q = BLK
    inner = z_ref.shape[1]
    cdim = xbc_ref.shape[1]
    gn = SSD_GROUPS * SSD_STATE
    b = pl.program_id(0)
    is_start, nvalid = _block_info(b, nbp, lreal, dec_seq)

    @pl.when(is_start)
    def _():
        win_ref[0:SUBLANES, :] = tail_ref[0]
        st_ref[0] = s0_ref[0]

    win_ref[SUBLANES:SUBLANES + q, :] = xbc_ref[...]
    row = lax.broadcasted_iota(jnp.int32, (q, 1), 0)
    valid = row < nvalid
    cchunk = 512
    for c0 in range(0, cdim, cchunk):
        acc = jnp.broadcast_to(cb_ref[:, c0:c0 + cchunk], (q, cchunk))
        for j in range(SSD_CONV):
            lo = SUBLANES - (SSD_CONV - 1) + j
            acc = acc + win_ref[lo:lo + q, c0:c0 + cchunk] * cw_ref[j:j + 1, c0:c0 + cchunk]
        xc = acc * _sigmoid(acc)
        xc_ref[:, c0:c0 + cchunk] = jnp.where(valid, xc, 0.0)
    win_ref[0:SUBLANES, :] = win_ref[q:q + SUBLANES, :]

    dt = _softplus(dt_ref[...] + dtb_ref[...])
    dt = jnp.where(valid, dt, 0.0)
    a = -jnp.exp(alog_ref[...])
    adt = dt * a
    ri = lax.broadcasted_iota(jnp.int32, (q, q), 0)
    ci = lax.broadcasted_iota(jnp.int32, (q, q), 1)
    causal = ri >= ci
    tri = causal.astype(F32)
    acs = jnp.dot(tri, adt, preferred_element_type=F32, precision=lax.Precision.HIGHEST)
    acs_t = acs.T
    last = acs[q - 1:q, :]
    lane_group = lax.broadcasted_iota(jnp.int32, (1, 4 * SSD_HEADDIM), 1) // SSD_HEADDIM

    ssq = jnp.zeros((q, 1), F32)
    for g in range(SSD_GROUPS):
        bb = xc_ref[:, inner + g * SSD_STATE:inner + (g + 1) * SSD_STATE]
        cc = xc_ref[:, inner + gn + g * SSD_STATE:inner + gn + (g + 1) * SSD_STATE]
        xs = xc_ref[:, g * 256:(g + 1) * 256]
        cbm = _bdot_nt(cc, bb)
        dt_x = _expand_heads(dt, g, lane_group)
        acs_x = _expand_heads(acs, g, lane_group)
        last_x = _expand_heads(last, g, lane_group)
        xdt = xs * dt_x
        st = st_ref[0, g]
        y = _bdot(cc, st) * jnp.exp(acs_x)
        for r in range(4):
            h = 4 * g + r
            seg = acs[:, h:h + 1] - acs_t[h:h + 1, :]
            lmat = jnp.where(causal, jnp.exp(jnp.where(causal, seg, 0.0)), 0.0)
            att = cbm * lmat
            y = y + _bdot(att, jnp.where(lane_group == r, xdt, 0.0))
        xdtd = xdt * jnp.exp(last_x - acs_x)
        st_ref[0, g] = st * jnp.exp(last_x) + _bdot(bb.T, xdtd)
        y = y + _expand_heads(d_ref[...], g, lane_group) * xs
        zz = z_ref[:, g * 256:(g + 1) * 256]
        y = y * (zz * _sigmoid(zz))
        ssq = ssq + jnp.sum(y * y, axis=-1, keepdims=True)
        yb_ref[:, g * 256:(g + 1) * 256] = y
    scale = lax.rsqrt(ssq / inner + SSD_NORM_EPS)
    y_ref[...] = (yb_ref[...] * scale * nw_ref[...]).astype(y_ref.dtype)


def _ssd_scan(xbc, z, dt, tails, s0t, conv_w, conv_b, dt_bias, a_log, d_skip, norm_w, lay):
    m, cdim = xbc.shape
    inner = z.shape[1]
    nstream = tails.shape[0]
    nh = dt_bias.shape[0]

    def pad_h(v):
        return jnp.pad(v.astype(F32), (0, LANES - nh)).reshape(1, LANES)

    body = functools.partial(_ssd_body, nbp=lay['nbp'], lreal=lay['lreal'], dec_seq=lay['dec_seq'])
    smap3 = _stream_map(lay['nbp'], 3)
    smap4 = _stream_map(lay['nbp'], 4)
    return pl.pallas_call(
        body,
        grid=(m // BLK,),
        in_specs=[pl.BlockSpec((BLK, cdim), lambda b: (b, 0)),
                  pl.BlockSpec((BLK, inner), lambda b: (b, 0)),
                  pl.BlockSpec((BLK, LANES), lambda b: (b, 0)),
                  pl.BlockSpec((1, SUBLANES, cdim), smap3),
                  pl.BlockSpec((1,) + s0t.shape[1:], smap4),
                  pl.BlockSpec((SSD_CONV, cdim), lambda b: (0, 0)),
                  pl.BlockSpec((1, cdim), lambda b: (0, 0)),
                  pl.BlockSpec((1, LANES), lambda b: (0, 0)),
                  pl.BlockSpec((1, LANES), lambda b: (0, 0)),
                  pl.BlockSpec((1, LANES), lambda b: (0, 0)),
                  pl.BlockSpec((1, inner), lambda b: (0, 0))],
        out_specs=[pl.BlockSpec((BLK, inner), lambda b: (b, 0)),
                   pl.BlockSpec((1,) + s0t.shape[1:], smap4)],
        out_shape=[jax.ShapeDtypeStruct((m, inner), BF16),
                   jax.ShapeDtypeStruct(s0t.shape, F32)],
        scratch_shapes=[pltpu.VMEM((SUBLANES + BLK, cdim), F32),
                        pltpu.VMEM((BLK, cdim), F32),
                        pltpu.VMEM((BLK, inner), F32)],
        compiler_params=_cp(),
        name='ssd_scan',
    )(xbc, z, dt, tails, s0t, conv_w, conv_b.reshape(1, cdim), pad_h(dt_bias), pad_h(a_log), pad_h(d_skip),
      norm_w.reshape(1, inner))


def _ffn_gate_body(gate_ref, up_ref, tail_ref, cw_ref, cb_ref, h_ref, win_ref, *, nbp, lreal, dec_seq):
    q = BLK
    b = pl.program_id(0)
    is_start, _ = _block_info(b, nbp, lreal, dec_seq)

    @pl.when(is_start)
    def _():
        win_ref[0:SUBLANES, :] = tail_ref[0]

    win_ref[SUBLANES:SUBLANES + q, :] = gate_ref[...]
    acc = jnp.broadcast_to(cb_ref[...], gate_ref.shape)
    for j in range(FFN_CONV):
        lo = SUBLANES - (FFN_CONV - 1) + j
        acc = acc + win_ref[lo:lo + q, :] * cw_ref[j:j + 1, :]
    h_ref[...] = (acc * _sigmoid(acc) * up_ref[...]).astype(h_ref.dtype)
    win_ref[0:SUBLANES, :] = win_ref[q:q + SUBLANES, :]


def _ffn_gate(gate, up, tails, conv_w, conv_b, lay):
    m, f = gate.shape
    body = functools.partial(_ffn_gate_body, nbp=lay['nbp'], lreal=lay['lreal'], dec_seq=lay['dec_seq'])
    return pl.pallas_call(
        body,
        grid=(m // BLK,),
        in_specs=[pl.BlockSpec((BLK, f), lambda b: (b, 0)),
                  pl.BlockSpec((BLK, f), lambda b: (b, 0)),
                  pl.BlockSpec((1, SUBLANES, f), _stream_map(lay['nbp'], 3)),
                  pl.BlockSpec((FFN_CONV, f), lambda b: (0, 0)),
                  pl.BlockSpec((1, f), lambda b: (0, 0))],
        out_specs=pl.BlockSpec((BLK, f), lambda b: (b, 0)),
        out_shape=jax.ShapeDtypeStruct((m, f), BF16),
        scratch_shapes=[pltpu.VMEM((SUBLANES + BLK, f), F32)],
        compiler_params=_cp(),
        name='ffn_gate',
    )(gate, up, tails, conv_w, conv_b.reshape(1, f))


def _rope_body(qk_ref, v_ref, cos_ref, sin_ref, q16_ref, k32_ref, k16_ref, v16_ref, vt_ref, *, d_model, scale):
    cos = cos_ref[...]
    sin = sin_ref[...]
    lane = lax.broadcasted_iota(jnp.int32, cos.shape, 1)
    first_half = (lane % DA_HD) < (DA_HD // 2)
    for c in range(2 * d_model // LANES):
        x = qk_ref[:, c * LANES:(c + 1) * LANES]
        partner = jnp.where(first_half, -pltpu.roll(x, LANES - DA_HD // 2, 1), pltpu.roll(x, DA_HD // 2, 1))
        out = x * cos + partner * sin
        if c < d_model // LANES:
            q16_ref[:, c * LANES:(c + 1) * LANES] = (out * scale).astype(BF16)
        else:
            c2 = c - d_model // LANES
            k32_ref[:, c2 * LANES:(c2 + 1) * LANES] = out
            k16_ref[:, c2 * LANES:(c2 + 1) * LANES] = out.astype(BF16)
    v = v_ref[...]
    v16_ref[...] = v.astype(BF16)
    for h in range(d_model // LANES):
        vt_ref[h, 0] = v[:, h * LANES:(h + 1) * LANES].T.astype(BF16)


def _rope(qkv, cos, sin, d_model):
    m = qkv.shape[0]
    t = ATT_TK
    nh = d_model // LANES
    return pl.pallas_call(
        functools.partial(_rope_body, d_model=d_model, scale=DA_HD ** -0.5 * math.log2(math.e)),
        grid=(m // t,),
        in_specs=[pl.BlockSpec((t, 2 * d_model), lambda i: (i, 0)),
                  pl.BlockSpec((t, d_model), lambda i: (i, 2)),
                  pl.BlockSpec((t, LANES), lambda i: (i, 0)),
                  pl.BlockSpec((t, LANES), lambda i: (i, 0))],
        out_specs=[pl.BlockSpec((t, d_model), lambda i: (i, 0))] * 4
        + [pl.BlockSpec((nh, 1, LANES, t), lambda i: (0, i, 0, 0))],
        out_shape=[jax.ShapeDtypeStruct((m, d_model), BF16),
                   jax.ShapeDtypeStruct((m, d_model), F32),
                   jax.ShapeDtypeStruct((m, d_model), BF16),
                   jax.ShapeDtypeStruct((m, d_model), BF16),
                   jax.ShapeDtypeStruct((nh, m // t, LANES, t), BF16)],
        compiler_params=_cp(),
        name='rope',
    )(qkv, qkv, cos, sin)


def _diff_lambda(lam_ref, lam_init):
    lp = lam_ref[...]
    return (jnp.exp(jnp.sum(lp[0:1] * lp[1:2], axis=-1, keepdims=True))
            - jnp.exp(jnp.sum(lp[2:3] * lp[3:4], axis=-1, keepdims=True)) + lam_init)


def _attn_prompt_body(q_ref, k_ref, vt_ref, lam_ref, g_ref, o_ref, acc_ref, *, tq, tk, lp, lreal, off, lam_init):
    qi = pl.program_id(1)
    nsub = tq // ATT_TQ_SUB
    chains = [(a, c) for a in range(nsub) for c in range(2)]
    qts, qchs = {}, {}
    for a in range(nsub):
        qt = q_ref[a * ATT_TQ_SUB:(a + 1) * ATT_TQ_SUB, :].astype(F32).T
        feat = lax.broadcasted_iota(jnp.int32, qt.shape, 0)
        qts[a, 0] = jnp.where(feat < DA_HD, qt, 0.0).astype(BF16)
        qts[a, 1] = jnp.where(feat >= DA_HD, qt, 0.0).astype(BF16)
        qpos = qi * tq + a * ATT_TQ_SUB + lax.broadcasted_iota(jnp.int32, (1, ATT_TQ_SUB), 1)
        qchs[a] = (qpos + off) >> 6
    acc_ref[...] = jnp.zeros_like(acc_ref)
    full_last = jnp.minimum(((qi * tq + off) >> 6) * CHUNK - off + CHUNK - 1, lreal - 1)
    any_last = jnp.minimum(((qi * tq + tq - 1 + off) >> 6) * CHUNK - off + CHUNK - 1, lreal - 1)
    nfull = (full_last + 1) // tk
    nk = any_last // tk + 1

    def tile(ki, carry, masked):
        ks = pl.multiple_of(ki * tk, tk)
        kt = k_ref[pl.ds(ks, tk), :]
        vt = vt_ref[0, ki]
        s = [jnp.dot(kt, qts[ch], preferred_element_type=F32) for ch in chains]
        if masked:
            kpos = ks + lax.broadcasted_iota(jnp.int32, (tk, 1), 0)
            for n, (a, _) in enumerate(chains):
                vis = jnp.logical_and(((kpos + off) >> 6) <= qchs[a], kpos < lreal)
                s[n] = jnp.where(vis, s[n], NEG_BIG)
        m_new = [jnp.maximum(carry[2 * n], jnp.max(s[n], axis=0, keepdims=True)) for n in range(len(chains))]
        p = [jnp.exp2(s[n] - m_new[n]) for n in range(len(chains))]
        new = []
        for n in range(len(chains)):
            alpha = jnp.exp2(carry[2 * n] - m_new[n])
            l_new = alpha * carry[2 * n + 1] + jnp.sum(p[n], axis=0, keepdims=True)
            acc_ref[n] = alpha * acc_ref[n] + jnp.dot(vt, p[n].astype(BF16), preferred_element_type=F32)
            new += [m_new[n], l_new]
        return tuple(new)

    carry = (jnp.full((1, ATT_TQ_SUB), NEG_BIG, F32), jnp.zeros((1, ATT_TQ_SUB), F32)) * len(chains)
    carry = lax.fori_loop(0, nfull, functools.partial(tile, masked=False), carry)
    carry = lax.fori_loop(nfull, nk, functools.partial(tile, masked=True), carry)
    lam = _diff_lambda(lam_ref, lam_init)
    for a in range(nsub):
        l0, l1 = carry[4 * a + 1], carry[4 * a + 3]
        o = acc_ref[2 * a] / l0 - lam * (acc_ref[2 * a + 1] / l1)
        ms = jnp.mean(o * o, axis=0, keepdims=True)
        o = o * lax.rsqrt(ms + SUBLN_EPS) * g_ref[...] * (1.0 - lam_init)
        o_ref[a * ATT_TQ_SUB:(a + 1) * ATT_TQ_SUB, :] = o.T.astype(o_ref.dtype)


def _attn_sample_body(q_ref, kn_ref, vn_ref, kc_ref, vc_ref, lam_ref, g_ref, o_alias_ref, o_ref, acc_ref, *,
                      tk, past, dec_seq, lam_init):
    del o_alias_ref
    nq = -(-dec_seq // 16) * 16
    q = q_ref[0:nq, :]
    lane = lax.broadcasted_iota(jnp.int32, q.shape, 1)
    zero = jnp.zeros_like(q)
    q2 = jnp.concatenate([jnp.where(lane < DA_HD, q, zero), jnp.where(lane >= DA_HD, q, zero)], axis=0)
    acc_ref[...] = jnp.zeros_like(acc_ref)
    qrow = lax.broadcasted_iota(jnp.int32, (2 * nq, 1), 0)
    qch = (past + jnp.where(qrow >= nq, qrow - nq, qrow)) >> 6

    def tile(kt, vt, vis, carry):
        m_old, l_old = carry
        s = lax.dot_general(q2, kt, _NT, preferred_element_type=F32)
        s = jnp.where(vis, s, NEG_BIG)
        m_new = jnp.maximum(m_old, jnp.max(s, axis=-1, keepdims=True))
        p = jnp.exp2(s - m_new)
        alpha = jnp.exp2(m_old - m_new)
        l_new = alpha * l_old + jnp.sum(p, axis=-1, keepdims=True)
        acc_ref[...] = alpha * acc_ref[...] + jnp.dot(p.astype(BF16), vt, preferred_element_type=F32)
        return m_new, l_new

    def body(ki, c):
        ks = pl.multiple_of(ki * tk, tk)
        kt = kc_ref[0, pl.ds(ks, tk), :].astype(BF16)
        vt = vc_ref[0, pl.ds(ks, tk), :].astype(BF16)
        kpos = ks + lax.broadcasted_iota(jnp.int32, (1, tk), 1)
        return tile(kt, vt, (kpos >> 6) <= qch, c)

    carry = (jnp.full((2 * nq, 1), NEG_BIG, F32), jnp.zeros((2 * nq, 1), F32))
    carry = lax.fori_loop(0, past // tk, body, carry)
    kidx = lax.broadcasted_iota(jnp.int32, (1, nq), 1)
    vis = jnp.logical_and(((past + kidx) >> 6) <= qch, kidx < dec_seq)
    _, l = tile(kn_ref[0:nq, :], vn_ref[0:nq, :], vis, carry)
    lam = _diff_lambda(lam_ref, lam_init)
    on = acc_ref[...] / l
    o = on[:nq] - lam * on[nq:]
    ms = jnp.mean(o * o, axis=-1, keepdims=True)
    o_ref[...] = jnp.zeros_like(o_ref)
    o_ref[0:nq, :] = (o * lax.rsqrt(ms + SUBLN_EPS) * g_ref[...] * (1.0 - lam_init)).astype(o_ref.dtype)


def _attention(q16, k16, v16, vt16, k_cache, v_cache, lam_p, subln_g, lam_init, lay):
    m, d = q16.shape
    nh = d // LANES
    lp = lay['lp']
    ns = lay['ns']
    past = k_cache.shape[1]
    g = subln_g.astype(F32)
    lam_p = lam_p.astype(F32)
    nkt = vt16.shape[1]
    o_prompt = pl.pallas_call(
        functools.partial(_attn_prompt_body, tq=ATT_TQ, tk=ATT_TK, lp=lp, lreal=lay['lreal'], off=CHUNK - N_META,
                          lam_init=lam_init),
        grid=(nh, lp // ATT_TQ),
        in_specs=[pl.BlockSpec((ATT_TQ, LANES), lambda h, i: (i, h)),
                  pl.BlockSpec((lp, LANES), lambda h, i: (0, h)),
                  pl.BlockSpec((1, nkt, LANES, ATT_TK), lambda h, i: (h, 0, 0, 0)),
                  pl.BlockSpec((4, DA_HD), lambda h, i: (0, 0)),
                  pl.BlockSpec((LANES, 1), lambda h, i: (0, 0))],
        out_specs=pl.BlockSpec((ATT_TQ, LANES), lambda h, i: (i, h)),
        out_shape=jax.ShapeDtypeStruct((m, d), BF16),
        scratch_shapes=[pltpu.VMEM((2 * ATT_TQ // ATT_TQ_SUB, LANES, ATT_TQ_SUB), F32)],
        compiler_params=_cp(),
        name='attn_prompt',
    )(q16, k16, vt16, lam_p, g.reshape(LANES, 1))
    nbp = lay['nbp']
    tk = min(512, past)
    nq = -(-lay['dec_seq'] // 16) * 16
    blk_spec = pl.BlockSpec((BLK, LANES), lambda s, h: (nbp + s, h))
    return pl.pallas_call(
        functools.partial(_attn_sample_body, tk=tk, past=past, dec_seq=lay['dec_seq'], lam_init=lam_init),
        grid=(ns, nh),
        in_specs=[blk_spec, blk_spec, blk_spec,
                  pl.BlockSpec((1, past, LANES), lambda s, h: (s, 0, h)),
                  pl.BlockSpec((1, past, LANES), lambda s, h: (s, 0, h)),
                  pl.BlockSpec((4, DA_HD), lambda s, h: (0, 0)),
                  pl.BlockSpec((1, LANES), lambda s, h: (0, 0)),
                  pl.BlockSpec(memory_space=pl.ANY)],
        out_specs=blk_spec,
        out_shape=jax.ShapeDtypeStruct((m, d), BF16),
        scratch_shapes=[pltpu.VMEM((2 * nq, LANES), F32)],
        input_output_aliases={7: 0},
        compiler_params=_cp(),
        name='attn_sample',
    )(q16, k16, v16, k_cache, v_cache, lam_p, g.reshape(1, LANES), o_prompt)


def _shift_mix_body(x_ref, tail_ref, mu_ref, *rest, nbp, lreal, dec_seq):
    outs, win_ref = rest[:-1], rest[-1]
    q = BLK
    b = pl.program_id(0)
    is_start, _ = _block_info(b, nbp, lreal, dec_seq)

    @pl.when(is_start)
    def _():
        win_ref[0:SUBLANES, :] = tail_ref[0]

    x = x_ref[...]
    win_ref[SUBLANES:SUBLANES + q, :] = x
    xx = win_ref[SUBLANES - 1:SUBLANES - 1 + q, :] - x
    for n, o_ref in enumerate(outs):
        o_ref[...] = (x + xx * mu_ref[n:n + 1, :]).astype(o_ref.dtype)
    win_ref[0:SUBLANES, :] = win_ref[q:q + SUBLANES, :]


def _shift_mix(x, tails, mu, lay):
    m, d = x.shape
    n = mu.shape[0]
    body = functools.partial(_shift_mix_body, nbp=lay['nbp'], lreal=lay['lreal'], dec_seq=lay['dec_seq'])
    return pl.pallas_call(
        body,
        grid=(m // BLK,),
        in_specs=[pl.BlockSpec((BLK, d), lambda b: (b, 0)),
                  pl.BlockSpec((1, SUBLANES, d), _stream_map(lay['nbp'], 3)),
                  pl.BlockSpec((n, d), lambda b: (0, 0))],
        out_specs=[pl.BlockSpec((BLK, d), lambda b: (b, 0))] * n,
        out_shape=[jax.ShapeDtypeStruct((m, d), BF16)] * n,
        scratch_shapes=[pltpu.VMEM((SUBLANES + BLK, d), F32)],
        compiler_params=_cp(),
        name='shift_mix',
    )(x, tails, mu)


def _half_sum(x, low):
    s_lo = jnp.sum(jnp.where(low, x, 0.0), axis=-1, keepdims=True)
    s_hi = jnp.sum(jnp.where(low, 0.0, x), axis=-1, keepdims=True)
    return jnp.where(low, s_lo, s_hi)


def _wkv_body(r_ref, k_ref, v_ref, lw_ref, la_ref, g_ref, s0_ref,
              w0_ref, a0_ref, kk_ref, ka_ref, lng_ref, lnb_ref, rk_ref,
              o_ref, st_ref, stage_ref, yb_ref, pw_ref, x_ref, ak_ref, dd_ref, es_ref, bke_ref, gc_ref,
              *, nbp, lreal, dec_seq):
    c = BLK
    npair = st_ref.shape[1]
    half = c // 2
    b = pl.program_id(0)
    is_start, nvalid = _block_info(b, nbp, lreal, dec_seq)

    @pl.when(is_start)
    def _():
        st_ref[0] = s0_ref[0]

    for i, ref in enumerate((r_ref, k_ref, v_ref, lw_ref, la_ref, g_ref)):
        for p in range(npair):
            stage_ref[i, p] = ref[:, p * LANES:(p + 1) * LANES]

    row = lax.broadcasted_iota(jnp.int32, (c, 1), 0)
    valid = row < nvalid
    lane = lax.broadcasted_iota(jnp.int32, (c, LANES), 1)
    low = lane < RK_HD
    ri = lax.broadcasted_iota(jnp.int32, (c, c), 0)
    ci = lax.broadcasted_iota(jnp.int32, (c, c), 1)
    lower_incl = ri >= ci
    lower_strict = ri > ci
    tri = lower_incl.astype(F32)
    eye = (ri == ci).astype(F32)
    same_head = (ri < RK_HD) == (ci < RK_HD)

    def prep(p, carry):
        r = stage_ref[0, p]
        k = stage_ref[1, p]
        v = jnp.where(valid, stage_ref[2, p], 0.0)
        lw = stage_ref[3, p]
        la = stage_ref[4, p]
        w_log = -_softplus(-(w0_ref[p] + lw)) - 0.5
        logw = jnp.where(valid, -jnp.exp(w_log), 0.0)
        ag = _sigmoid(a0_ref[p] + la)
        kk = k * kk_ref[p]
        kk = kk / jnp.maximum(jnp.sqrt(_half_sum(kk * kk, low)), 1e-12)
        kx = k * (1.0 + (ag - 1.0) * ka_ref[p])
        a = jnp.where(valid, -kk, 0.0)
        bv = jnp.where(valid, kk * ag, 0.0)
        kxm = jnp.where(valid, kx, 0.0)

        cum = jnp.dot(tri, logw, preferred_element_type=F32, precision=lax.Precision.HIGHEST)
        cprev = cum - logw
        cmid = cum[half - 1:half, :]
        cend = cum[c - 1:c, :]
        a_abs = a * jnp.exp(cprev)
        r_abs = r * jnp.exp(cum)
        a_rel = a * jnp.exp(cprev - cmid)
        r_rel = r * jnp.exp(cum - cmid)
        e_inv = jnp.exp(cmid - cum)
        b_rel = bv * e_inv
        k_rel = kxm * e_inv
        e_end = jnp.exp(cend - cum)
        bk_end = jnp.concatenate([bv * e_end, kxm * e_end], axis=0)

        stage_ref[1, p] = kxm
        stage_ref[2, p] = v
        s = st_ref[0, p]
        es_ref[p] = _bdot_nt(jnp.concatenate([a_abs, r_abs], axis=0), s)
        bke_ref[p] = bk_end.astype(BF16)
        gc_ref[p] = jnp.broadcast_to(jnp.exp(cend), (SUBLANES, LANES))
        bk_rel = jnp.concatenate([b_rel, k_rel], axis=0).astype(BF16)
        for j, sel in enumerate((low, jnp.logical_not(low))):
            ar = jnp.concatenate([jnp.where(sel, a_rel, 0.0), jnp.where(sel, r_rel, 0.0)], axis=0)
            mm = lax.dot_general(ar.astype(BF16), bk_rel, _NT, preferred_element_type=F32)
            n = jnp.where(lower_strict, mm[:c, :c], 0.0)
            h = 2 * p + j
            pw_ref[h] = n.astype(BF16)
            x_ref[h] = eye + n
            ak_ref[h] = jnp.where(lower_strict, mm[:c, c:], 0.0).astype(BF16)
            dd_ref[h] = jnp.concatenate([jnp.where(lower_incl, mm[c:, :c], 0.0),
                                         jnp.where(lower_incl, mm[c:, c:], 0.0)], axis=1).astype(BF16)
        return carry

    lax.fori_loop(0, npair, prep, 0)

    nlev = int(math.log2(c))
    inv_group = 8

    def invert(g, carry):
        hs = [g * inv_group + j for j in range(inv_group)]
        for h in hs:
            pw = pw_ref[h]
            pw_ref[h] = jnp.dot(pw, pw, preferred_element_type=F32).astype(BF16)
        for _ in range(1, nlev - 1):
            for h in hs:
                pw = pw_ref[h]
                both = jnp.dot(jnp.concatenate([pw, x_ref[h].astype(BF16)], axis=0), pw,
                               preferred_element_type=F32)
                pw_ref[h] = both[:c].astype(BF16)
                x_ref[h] = x_ref[h] + both[c:]
        for h in hs:
            x = x_ref[h]
            x_ref[h] = x + jnp.dot(x.astype(BF16), pw_ref[h], preferred_element_type=F32)
        return carry

    lax.fori_loop(0, 2 * npair // inv_group, invert, 0)

    out_group = 4

    def emit(g, carry):
        ps = [g * out_group + i for i in range(out_group)]
        hs = [(i, j) for i in range(out_group) for j in range(2)]
        rhs, us, ys = {}, {}, {}
        for i, j in hs:
            p = ps[i]
            rhs[i, j] = es_ref[p, 0:c, :] + jnp.dot(ak_ref[2 * p + j], stage_ref[2, p].astype(BF16),
                                                    preferred_element_type=F32)
        for i, j in hs:
            us[i, j] = _bdot(x_ref[2 * ps[i] + j], rhs[i, j])
        for i, j in hs:
            p = ps[i]
            uv = jnp.concatenate([us[i, j], stage_ref[2, p]], axis=0).astype(BF16)
            ys[i, j] = es_ref[p, c:2 * c, :] + jnp.dot(dd_ref[2 * p + j], uv, preferred_element_type=F32)
        for i, p in enumerate(ps):
            v = stage_ref[2, p]
            u = jnp.where(low, us[i, 0], us[i, 1])
            y = jnp.where(low, ys[i, 0], ys[i, 1])
            uv_t = jnp.concatenate([u, v], axis=0).T.astype(BF16)
            upd = jnp.dot(uv_t, bke_ref[p], preferred_element_type=F32)
            st_ref[0, p] = st_ref[0, p] * gc_ref[p, 0:1, :] + jnp.where(same_head, upd, 0.0)

            mean = _half_sum(y, low) * (1.0 / RK_HD)
            dy = y - mean
            var = _half_sum(dy * dy, low) * (1.0 / RK_HD)
            yn = dy * lax.rsqrt(var + RK_GN_EPS) * lng_ref[p] + lnb_ref[p]
            bonus = _half_sum(stage_ref[0, p] * stage_ref[1, p] * rk_ref[p], low) * v
            yb_ref[p] = ((yn + bonus) * stage_ref[5, p]).astype(yb_ref.dtype)
        return carry

    lax.fori_loop(0, npair // out_group, emit, 0)
    for p in range(npair):
        o_ref[:, p * LANES:(p + 1) * LANES] = yb_ref[p]


def _wkv(r, k, v, lw, la, gate, s0, w0, a0, k_k, k_a, lnx_g, lnx_b, r_k, lay):
    m, d = r.shape
    npair = d // LANES

    def prm(t):
        return t.astype(F32).reshape(npair, 1, LANES)

    body = functools.partial(_wkv_body, nbp=lay['nbp'], lreal=lay['lreal'], dec_seq=lay['dec_seq'])
    row_spec = pl.BlockSpec((BLK, d), lambda b: (b, 0))
    prm_spec = pl.BlockSpec((npair, 1, LANES), lambda b: (0, 0, 0))
    smap = _stream_map(lay['nbp'], 4)
    return pl.pallas_call(
        body,
        grid=(m // BLK,),
        in_specs=[row_spec] * 6 + [pl.BlockSpec((1,) + s0.shape[1:], smap)] + [prm_spec] * 7,
        out_specs=[row_spec, pl.BlockSpec((1,) + s0.shape[1:], smap)],
        out_shape=[jax.ShapeDtypeStruct((m, d), BF16), jax.ShapeDtypeStruct(s0.shape, F32)],
        scratch_shapes=[pltpu.VMEM((6, npair, BLK, LANES), F32),
                        pltpu.VMEM((npair, BLK, LANES), BF16),
                        pltpu.VMEM((2 * npair, BLK, BLK), BF16),
                        pltpu.VMEM((2 * npair, BLK, BLK), F32),
                        pltpu.VMEM((2 * npair, BLK, BLK), BF16),
                        pltpu.VMEM((2 * npair, BLK, 2 * BLK), BF16),
                        pltpu.VMEM((npair, 2 * BLK, LANES), F32),
                        pltpu.VMEM((npair, 2 * BLK, LANES), BF16),
                        pltpu.VMEM((npair, SUBLANES, LANES), F32)],
        compiler_params=_cp(),
        name='wkv7',
    )(r, k, v, lw, la, gate, s0, prm(w0), prm(a0), prm(k_k), prm(k_a), prm(lnx_g), prm(lnx_b), prm(r_k))


def _tails(prev, nstream):
    ns, w, c = prev.shape
    t = jnp.zeros((nstream, SUBLANES, c), F32)
    return t.at[1:, SUBLANES - w:, :].set(prev.astype(F32))


def _last_rows(flat, w, lay):
    lreal, lp, ns, dec = lay['lreal'], lay['lp'], lay['ns'], lay['dec_seq']
    p = flat[lreal - w:lreal][None]
    s = flat[lp:].reshape(ns, BLK, -1)[:, dec - w:dec]
    return p, s


def kernel(x_prompt, x_sample, cache_attn_k, cache_attn_v, state_ssm, state_ssd_conv, state_wkv, state_rwkv_shift, state_ffn_conv, meta_tokens, ssd_w_in, ssd_conv_w, ssd_conv_b, ssd_dt_bias, ssd_a_log, ssd_d, ssd_norm_w, ssd_w_out, da_w_qkv, da_lambda, da_subln_g, da_w_o, rk_mu, rk_w0, rk_w1, rk_w2, rk_a0, rk_a1, rk_a2, rk_g1, rk_g2, rk_k_k, rk_k_a, rk_r_k, rk_w_r, rk_w_k, rk_w_v, rk_w_o, rk_lnx_g, rk_lnx_b, ffn_w_up, ffn_w_gate, ffn_conv_w, ffn_conv_b, ffn_w_down, ln_g, ln_b):
    bp, seq, d = x_prompt.shape
    ns, dec_seq, _ = x_sample.shape
    depth = ln_g.shape[0]
    past = cache_attn_k.shape[2]
    assert bp == 1 and dec_seq >= SSD_CONV - 1 and dec_seq <= BLK and past % min(512, past) == 0
    lreal = N_META + seq
    row_align = max(TM, ATT_TQ, ATT_TK)
    lp = -(-lreal // row_align) * row_align
    m = lp + ns * BLK
    assert m % row_align == 0
    nstream = ns + 1
    lay = dict(lreal=lreal, lp=lp, nbp=lp // BLK, ns=ns, dec_seq=dec_seq)
    alpha = (2 * depth) ** 0.25

    x = jnp.concatenate([
        meta_tokens.astype(F32), x_prompt[0], jnp.zeros((lp - lreal, d), F32),
        jnp.pad(x_sample, ((0, 0), (0, BLK - dec_seq), (0, 0))).reshape(ns * BLK, d)], axis=0)

    pos = jnp.concatenate([jnp.arange(lp), jnp.tile(past + jnp.arange(BLK), ns)]).astype(F32)
    half = DA_HD // 2
    inv = ROPE_THETA ** (-jnp.arange(half, dtype=F32) / half)
    ang = pos[:, None] * inv[None, :]
    cos_t = jnp.tile(jnp.cos(ang), (1, LANES // half))
    sin_t = jnp.tile(jnp.sin(ang), (1, LANES // half))

    inner = ssd_norm_w.shape[1]
    nheads = ssd_dt_bias.shape[1]
    hpg = nheads // SSD_GROUPS
    new = {n: [] for n in ('k', 'v', 'ssm', 'ssd_conv', 'wkv', 'shift', 'ffn_conv')}

    for i in range(depth):
        j, kind = i // N_MIXERS, i % N_MIXERS
        if kind == 0:
            w_in = ssd_w_in[j]
            cdim = ssd_conv_w.shape[2]
            w_z = w_in[:, :inner].astype(BF16)
            w_xbc = w_in[:, inner:inner + cdim].astype(BF16)
            w_dt = jnp.pad(w_in[:, inner + cdim:], ((0, 0), (0, LANES - nheads))).astype(BF16)
            z, dt = _mm_multi(x, [w_z, w_dt], [F32, F32], 'ssd_in_z_dt')
            (xbc,) = _mm_multi(x, [w_xbc], [F32], 'ssd_in_xbc')
            s0 = jnp.concatenate([jnp.zeros((1,) + state_ssm.shape[2:], F32), state_ssm[j].astype(F32)], axis=0)
            s0t = s0.reshape(nstream, SSD_GROUPS, hpg * SSD_HEADDIM, SSD_STATE).swapaxes(2, 3)
            y, st = _ssd_scan(xbc, z, dt, _tails(state_ssd_conv[j], nstream), s0t, ssd_conv_w[j], ssd_conv_b[j],
                              ssd_dt_bias[j], ssd_a_log[j], ssd_d[j], ssd_norm_w[j], lay)
            w_o = ssd_w_out[j].astype(BF16)
            new['ssd_conv'].append(_last_rows(xbc, SSD_CONV - 1, lay))
            ssm = st.swapaxes(2, 3).reshape(nstream, nheads, SSD_HEADDIM, SSD_STATE)
            new['ssm'].append((ssm[:1], ssm[1:]))
        elif kind == 1:
            lam_init = 0.8 - 0.6 * math.exp(-0.3 * i)
            (qkv,) = _mm_multi(x, [da_w_qkv[j].astype(BF16)], [F32], 'da_qkv')
            q16, k32, k16, v16, vt16 = _rope(qkv, cos_t, sin_t, d)
            nh = d // (2 * DA_HD)
            y = _attention(q16, k16, v16, vt16, cache_attn_k[j].reshape(ns, past, d),
                           cache_attn_v[j].reshape(ns, past, d), da_lambda[j], da_subln_g[j], lam_init, lay)
            w_o = da_w_o[j].astype(BF16)
            kp, ks = k32[:lreal][None], k32[lp:].reshape(ns, BLK, d)[:, :dec_seq]
            v32 = qkv[:, 2 * d:]
            vp, vs = v32[:lreal][None], v32[lp:].reshape(ns, BLK, d)[:, :dec_seq]
            new['k'].append((kp.reshape(1, lreal, nh, 2, DA_HD), ks.reshape(ns, dec_seq, nh, 2, DA_HD)))
            new['v'].append((vp.reshape(1, lreal, nh, 2 * DA_HD), vs.reshape(ns, dec_seq, nh, 2 * DA_HD)))
        else:
            xr, xw, xk, xv, xa, xg = _shift_mix(x, _tails(state_rwkv_shift[j], nstream), rk_mu[j], lay)
            (r,) = _mm_multi(xr, [rk_w_r[j].astype(BF16)], [F32], 'rk_r')
            (k,) = _mm_multi(xk, [rk_w_k[j].astype(BF16)], [F32], 'rk_k')
            (v,) = _mm_multi(xv, [rk_w_v[j].astype(BF16)], [F32], 'rk_v')
            lw = _lora(xw, rk_w1[j], rk_w2[j], 'tanh', 'rk_lora_w')
            la = _lora(xa, rk_a1[j], rk_a2[j], 'none', 'rk_lora_a')
            gate = _lora(xg, rk_g1[j], rk_g2[j], 'sigmoid', 'rk_lora_g')
            nrh = d // RK_HD
            npair = nrh // 2
            sw = jnp.concatenate([jnp.zeros((1,) + state_wkv.shape[2:], F32), state_wkv[j].astype(F32)], axis=0)
            sw = sw.reshape(nstream, npair, 2, RK_HD, RK_HD)
            zero = jnp.zeros_like(sw[:, :, 0])
            s0 = jnp.concatenate([jnp.concatenate([sw[:, :, 0], zero], axis=-1),
                                  jnp.concatenate([zero, sw[:, :, 1]], axis=-1)], axis=-2)
            y, st = _wkv(r, k, v, lw, la, gate, s0, rk_w0[j], rk_a0[j], rk_k_k[j], rk_k_a[j], rk_lnx_g[j],
                         rk_lnx_b[j], rk_r_k[j].reshape(-1), lay)
            w_o = rk_w_o[j].astype(BF16)
            wkv = jnp.stack([st[:, :, :RK_HD, :RK_HD], st[:, :, RK_HD:, RK_HD:]], axis=2).reshape(
                nstream, nrh, RK_HD, RK_HD)
            new['wkv'].append((wkv[:1], wkv[1:]))
            new['shift'].append(_last_rows(x, 1, lay))
        x = _mm_ln(y, w_o, x, ln_g[i, 0], ln_b[i, 0], alpha, 'mixer_out_ln')
        gate_pre, up = _mm_multi(x, [ffn_w_gate[i].astype(BF16), ffn_w_up[i].astype(BF16)], [F32, F32], 'ffn_in')
        h = _ffn_gate(gate_pre, up, _tails(state_ffn_conv[i], nstream), ffn_conv_w[i], ffn_conv_b[i], lay)
        new['ffn_conv'].append(_last_rows(gate_pre, FFN_CONV - 1, lay))
        x = _mm_ln(h, ffn_w_down[i].astype(BF16), x, ln_g[i, 1], ln_b[i, 1], alpha, 'ffn_out_ln')

    y_prompt = x[N_META:lreal][None]
    y_sample = x[lp:].reshape(ns, BLK, d)[:, :dec_seq]
    order = ('k', 'v', 'ssm', 'ssd_conv', 'wkv', 'shift', 'ffn_conv')
    prompt_states = tuple(jnp.stack([t[0] for t in new[n]]) for n in order)
    sample_states = tuple(jnp.stack([t[1] for t in new[n]]) for n in order)
    return (y_prompt, y_sample) + prompt_states + sample_states
```

```python
import functools
import math

import jax
import jax.numpy as jnp
from jax import lax
from jax.experimental import pallas as pl
from jax.experimental.pallas import tpu as pltpu

F32 = jnp.float32
BF16 = jnp.bfloat16

CHUNK = 64
N_META = 16
N_MIXERS = 3
SSD_HEADDIM = 64
SSD_GROUPS = 8
SSD_STATE = 128
SSD_CONV = 4
SSD_NORM_EPS = 1e-5
DA_HD = 64
ROPE_THETA = 10000.0
SUBLN_EPS = 1e-5
RK_HD = 64
RK_GN_EPS = 64e-5
FFN_CONV = 3
LN_EPS = 1e-5

LANES = 128
SUBLANES = 8
BLK = 128
TM = 512
ATT_TQ = 512
ATT_TQ_SUB = 256
ATT_TK = 512
VMEM_LIMIT = 56 * 1024 * 1024
NEG_BIG = -1e30

_NT = (((1,), (1,)), ((), ()))


def _cp():
    return pltpu.CompilerParams(vmem_limit_bytes=VMEM_LIMIT)


def _sigmoid(x):
    return 0.5 + 0.5 * jnp.tanh(0.5 * x)


def _softplus(x):
    return jnp.maximum(x, 0.0) + jnp.log(1.0 + jnp.exp(-jnp.abs(x)))


def _bdot(a, b):
    return jnp.dot(a.astype(BF16), b.astype(BF16), preferred_element_type=F32)


def _bdot_nt(a, b):
    return lax.dot_general(a.astype(BF16), b.astype(BF16), _NT, preferred_element_type=F32)


def _mm_multi_body(*refs, n):
    x = refs[0][...].astype(BF16)
    for i in range(n):
        refs[1 + n + i][...] = jnp.dot(x, refs[1 + i][...], preferred_element_type=F32).astype(refs[1 + n + i].dtype)


def _mm_multi(x, ws, out_dtypes, name):
    m, k = x.shape
    n = len(ws)
    outs = pl.pallas_call(
        functools.partial(_mm_multi_body, n=n),
        grid=(m // TM,),
        in_specs=[pl.BlockSpec((TM, k), lambda i: (i, 0))]
        + [pl.BlockSpec(w.shape, lambda i: (0, 0), pipeline_mode=pl.Buffered(1)) for w in ws],
        out_specs=[pl.BlockSpec((TM, w.shape[1]), lambda i: (i, 0)) for w in ws],
        out_shape=[jax.ShapeDtypeStruct((m, w.shape[1]), dt) for w, dt in zip(ws, out_dtypes)],
        compiler_params=_cp(),
        name=name,
    )(x, *ws)
    return outs


def _mm_ln_body(x_ref, w_ref, r_ref, g_ref, b_ref, o_ref, *, alpha):
    acc = jnp.dot(x_ref[...].astype(BF16), w_ref[...], preferred_element_type=F32)
    h = alpha * r_ref[...] + acc
    mu = jnp.mean(h, axis=-1, keepdims=True)
    d = h - mu
    var = jnp.mean(d * d, axis=-1, keepdims=True)
    o_ref[...] = d * lax.rsqrt(var + LN_EPS) * g_ref[...] + b_ref[...]


def _mm_ln(x, w, resid, g, b, alpha, name):
    m, k = x.shape
    d = w.shape[1]
    return pl.pallas_call(
        functools.partial(_mm_ln_body, alpha=alpha),
        grid=(m // TM,),
        in_specs=[pl.BlockSpec((TM, k), lambda i: (i, 0)),
                  pl.BlockSpec((k, d), lambda i: (0, 0), pipeline_mode=pl.Buffered(1)),
                  pl.BlockSpec((TM, d), lambda i: (i, 0)),
                  pl.BlockSpec((1, d), lambda i: (0, 0)),
                  pl.BlockSpec((1, d), lambda i: (0, 0))],
        out_specs=pl.BlockSpec((TM, d), lambda i: (i, 0)),
        out_shape=jax.ShapeDtypeStruct((m, d), F32),
        compiler_params=_cp(),
        name=name,
    )(x, w, resid, g.reshape(1, d), b.reshape(1, d))


def _lora_body(x_ref, w1_ref, w2_ref, o_ref, *, act):
    h = jnp.dot(x_ref[...], w1_ref[...], preferred_element_type=F32)
    if act == 'tanh':
        h = jnp.tanh(h)
    elif act == 'sigmoid':
        h = _sigmoid(h)
    o_ref[...] = jnp.dot(h.astype(BF16), w2_ref[...], preferred_element_type=F32)


def _lora(x, w1, w2, act, name):
    m, k = x.shape
    r = w1.shape[1]
    rp = -(-r // LANES) * LANES
    w1p = jnp.pad(w1, ((0, 0), (0, rp - r))).astype(BF16)
    w2p = jnp.pad(w2, ((0, rp - r), (0, 0))).astype(BF16)
    d = w2.shape[1]
    return pl.pallas_call(
        functools.partial(_lora_body, act=act),
        grid=(m // TM,),
        in_specs=[pl.BlockSpec((TM, k), lambda i: (i, 0)),
                  pl.BlockSpec((k, rp), lambda i: (0, 0)),
                  pl.BlockSpec((rp, d), lambda i: (0, 0))],
        out_specs=pl.BlockSpec((TM, d), lambda i: (i, 0)),
        out_shape=jax.ShapeDtypeStruct((m, d), F32),
        compiler_params=_cp(),
        name=name,
    )(x, w1p, w2p)


def _block_info(b, nbp, lreal, dec_seq):
    is_start = jnp.logical_or(b == 0, b >= nbp)
    nvalid = jnp.where(b < nbp, jnp.clip(lreal - b * BLK, 0, BLK), dec_seq)
    return is_start, nvalid


def _stream_map(nbp, ndim):
    def index_map(b):
        return (jnp.maximum(b - nbp + 1, 0),) + (0,) * (ndim - 1)
    return index_map


def _expand_heads(colmat, g, lane_group):
    out = colmat[:, 4 * g + 3:4 * g + 4]
    for r in (2, 1, 0):
        out = jnp.where(lane_group == r, colmat[:, 4 * g + r:4 * g + r + 1], out)
    return out


def _ssd_body(xc_ref, z_ref, dt_ref, s0_ref, dtb_ref, alog_ref, d_ref, nw_ref,
              y_ref, st_ref, yb_ref, *, nbp, lreal, dec_seq):
    q = BLK
    inner = z_ref.shape[1]
    gn = SSD_GROUPS * SSD_STATE
    b = pl.program_id(0)
    is_start, nvalid = _block_info(b, nbp, lreal, dec_seq)

    @pl.when(is_start)
    def _():
        st_ref[0] = s0_ref[0]

    row = lax.broadcasted_iota(jnp.int32, (q, 1), 0)
    valid = row < nvalid
    dt = _softplus(dt_ref[...] + dtb_ref[...])
    dt = jnp.where(valid, dt, 0.0)
    a = -jnp.exp(alog_ref[...])
    adt = dt * a
    ri = lax.broadcasted_iota(jnp.int32, (q, q), 0)
    ci = lax.broadcasted_iota(jnp.int32, (q, q), 1)
    causal = ri >= ci
    tri = causal.astype(F32)
    acs = jnp.dot(tri, adt, preferred_element_type=F32, precision=lax.Precision.HIGHEST)
    acs_t = acs.T
    last = acs[q - 1:q, :]
    lane_group = lax.broadcasted_iota(jnp.int32, (1, 4 * SSD_HEADDIM), 1) // SSD_HEADDIM

    ssq = jnp.zeros((q, 1), F32)
    for g in range(SSD_GROUPS):
        bb = xc_ref[:, inner + g * SSD_STATE:inner + (g + 1) * SSD_STATE]
        cc = xc_ref[:, inner + gn + g * SSD_STATE:inner + gn + (g + 1) * SSD_STATE]
        xs = xc_ref[:, g * 256:(g + 1) * 256]
        cbm = _bdot_nt(cc, bb)
        dt_x = _expand_heads(dt, g, lane_group)
        acs_x = _expand_heads(acs, g, lane_group)
        last_x = _expand_heads(last, g, lane_group)
        xdt = xs * dt_x
        st = st_ref[0, g]
        y = _bdot(cc, st) * jnp.exp(acs_x)
        for r in range(4):
            h = 4 * g + r
            seg = acs[:, h:h + 1] - acs_t[h:h + 1, :]
            lmat = jnp.where(causal, jnp.exp(jnp.where(causal, seg, 0.0)), 0.0)
            att = cbm * lmat
            y = y + _bdot(att, jnp.where(lane_group == r, xdt, 0.0))
        xdtd = xdt * jnp.exp(last_x - acs_x)
        st_ref[0, g] = st * jnp.exp(last_x) + _bdot(bb.T, xdtd)
        y = y + _expand_heads(d_ref[...], g, lane_group) * xs
        zz = z_ref[:, g * 256:(g + 1) * 256]
        y = y * (zz * _sigmoid(zz))
        ssq = ssq + jnp.sum(y * y, axis=-1, keepdims=True)
        yb_ref[:, g * 256:(g + 1) * 256] = y
    scale = lax.rsqrt(ssq / inner + SSD_NORM_EPS)
    y_ref[...] = (yb_ref[...] * scale * nw_ref[...]).astype(y_ref.dtype)


def _ssd_scan(xc, z, dt, s0t, dt_bias, a_log, d_skip, norm_w, lay):
    m, cdim = xc.shape
    inner = z.shape[1]
    nh = dt_bias.shape[0]

    def pad_h(v):
        return jnp.pad(v.astype(F32), (0, LANES - nh)).reshape(1, LANES)

    body = functools.partial(_ssd_body, nbp=lay['nbp'], lreal=lay['lreal'], dec_seq=lay['dec_seq'])
    smap4 = _stream_map(lay['nbp'], 4)
    return pl.pallas_call(
        body,
        grid=(m // BLK,),
        in_specs=[pl.BlockSpec((BLK, cdim), lambda b: (b, 0)),
                  pl.BlockSpec((BLK, inner), lambda b: (b, 0)),
                  pl.BlockSpec((BLK, LANES), lambda b: (b, 0)),
                  pl.BlockSpec((1,) + s0t.shape[1:], smap4),
                  pl.BlockSpec((1, LANES), lambda b: (0, 0)),
                  pl.BlockSpec((1, LANES), lambda b: (0, 0)),
                  pl.BlockSpec((1, LANES), lambda b: (0, 0)),
                  pl.BlockSpec((1, inner), lambda b: (0, 0))],
        out_specs=[pl.BlockSpec((BLK, inner), lambda b: (b, 0)),
                   pl.BlockSpec((1,) + s0t.shape[1:], smap4)],
        out_shape=[jax.ShapeDtypeStruct((m, inner), BF16),
                   jax.ShapeDtypeStruct(s0t.shape, F32)],
        scratch_shapes=[pltpu.VMEM((BLK, inner), F32)],
        compiler_params=_cp(),
        name='ssd_scan',
    )(xc, z, dt, s0t, pad_h(dt_bias), pad_h(a_log), pad_h(d_skip), norm_w.reshape(1, inner))


def _conv_taps(a, prev8, cw_ref, cb, cs, width):
    ext = jnp.concatenate([prev8, a], axis=0)
    out = cb + a * cw_ref[width - 1:width, cs]
    for s in range(1, width):
        shifted = ext[SUBLANES - s:SUBLANES - s + a.shape[0]]
        out = out + shifted * cw_ref[width - 1 - s:width - s, cs]
    return out


def _mm_conv_body(*refs, width, gated, nbp, lreal, dec_seq, tm, chunk):
    if gated:
        x_ref, w_ref, wu_ref, tail_ref, cw_ref, cb_ref, o_ref, st_ref, carry_ref = refs
    else:
        x_ref, w_ref, tail_ref, cw_ref, cb_ref, o_ref, st_ref, carry_ref = refs
    i = pl.program_id(0)
    nsub = tm // BLK
    n = w_ref.shape[1]
    nv_last = lreal - ((lreal - 1) // BLK) * BLK
    x = x_ref[...].astype(BF16)
    row = lax.broadcasted_iota(jnp.int32, (BLK, 1), 0)

    @pl.when(i == 0)
    def _():
        st_ref[...] = jnp.zeros_like(st_ref)
        carry_ref[...] = jnp.zeros_like(carry_ref)

    for c0 in range(0, n, chunk):
        cs = slice(c0, c0 + chunk)
        acc = jnp.dot(x, w_ref[:, cs], preferred_element_type=F32)
        if gated:
            up = jnp.dot(x, wu_ref[:, cs], preferred_element_type=F32)
        for j in range(nsub):
            blk = i * nsub + j
            is_start, nvalid = _block_info(blk, nbp, lreal, dec_seq)
            sid = jnp.maximum(blk - nbp + 1, 0)
            rows = slice(j * BLK, (j + 1) * BLK)
            a = acc[rows]
            before = carry_ref[:, cs] if j == 0 else acc[j * BLK - SUBLANES:j * BLK]
            prev8 = jnp.where(is_start, tail_ref[sid, :, cs], before)
            conv = _conv_taps(a, prev8, cw_ref, cb_ref[:, cs], cs, width)
            act = conv * _sigmoid(conv)
            if gated:
                o_ref[rows, cs] = (act * up[rows]).astype(o_ref.dtype)
            else:
                o_ref[rows, cs] = jnp.where(row < nvalid, act, 0.0).astype(o_ref.dtype)
            both = jnp.concatenate([prev8, a], axis=0)
            cand = jnp.where(blk >= nbp, both[dec_seq:dec_seq + SUBLANES],
                             jnp.where(nvalid == BLK, both[BLK:BLK + SUBLANES], both[nv_last:nv_last + SUBLANES]))
            st_ref[sid, :, cs] = jnp.where(nvalid > 0, cand, st_ref[sid, :, cs])
        carry_ref[:, cs] = acc[tm - SUBLANES:tm]


def _mm_conv(x, w, wu, tails, conv_w, conv_b, out_dtype, lay, name):
    m, k = x.shape
    n = w.shape[1]
    width = conv_w.shape[0]
    nstream = tails.shape[0]
    gated = wu is not None
    body = functools.partial(_mm_conv_body, width=width, gated=gated, nbp=lay['nbp'], lreal=lay['lreal'],
                             dec_seq=lay['dec_seq'], tm=TM, chunk=2 * LANES)
    w_spec = pl.BlockSpec((k, n), lambda i: (0, 0), pipeline_mode=pl.Buffered(1))
    full = lambda shape: pl.BlockSpec(shape, lambda i: (0,) * len(shape))
    return pl.pallas_call(
        body,
        grid=(m // TM,),
        in_specs=[pl.BlockSpec((TM, k), lambda i: (i, 0)), w_spec] + ([w_spec] if gated else [])
        + [full((nstream, SUBLANES, n)), full((width, n)), full((1, n))],
        out_specs=[pl.BlockSpec((TM, n), lambda i: (i, 0)), full((nstream, SUBLANES, n))],
        out_shape=[jax.ShapeDtypeStruct((m, n), out_dtype), jax.ShapeDtypeStruct((nstream, SUBLANES, n), F32)],
        scratch_shapes=[pltpu.VMEM((SUBLANES, n), F32)],
        compiler_params=_cp(),
        name=name,
    )(*([x, w] + ([wu] if gated else []) + [tails, conv_w, conv_b.reshape(1, n)]))


def _rope_body(qk_ref, v_ref, cos_ref, sin_ref, q16_ref, k32_ref, k16_ref, v16_ref, vt_ref, *, d_model, scale):
    cos = cos_ref[...]
    sin = sin_ref[...]
    lane = lax.broadcasted_iota(jnp.int32, cos.shape, 1)
    first_half = (lane % DA_HD) < (DA_HD // 2)
    for c in range(2 * d_model // LANES):
        x = qk_ref[:, c * LANES:(c + 1) * LANES]
        partner = jnp.where(first_half, -pltpu.roll(x, LANES - DA_HD // 2, 1), pltpu.roll(x, DA_HD // 2, 1))
        out = x * cos + partner * sin
        if c < d_model // LANES:
            q16_ref[:, c * LANES:(c + 1) * LANES] = (out * scale).astype(BF16)
        else:
            c2 = c - d_model // LANES
            k32_ref[:, c2 * LANES:(c2 + 1) * LANES] = out
            k16_ref[:, c2 * LANES:(c2 + 1) * LANES] = out.astype(BF16)
    v = v_ref[...]
    v16_ref[...] = v.astype(BF16)
    for h in range(d_model // LANES):
        vt_ref[h, 0] = v[:, h * LANES:(h + 1) * LANES].T.astype(BF16)


def _rope(qkv, cos, sin, d_model):
    m = qkv.shape[0]
    t = ATT_TK
    nh = d_model // LANES
    return pl.pallas_call(
        functools.partial(_rope_body, d_model=d_model, scale=DA_HD ** -0.5 * math.log2(math.e)),
        grid=(m // t,),
        in_specs=[pl.BlockSpec((t, 2 * d_model), lambda i: (i, 0)),
                  pl.BlockSpec((t, d_model), lambda i: (i, 2)),
                  pl.BlockSpec((t, LANES), lambda i: (i, 0)),
                  pl.BlockSpec((t, LANES), lambda i: (i, 0))],
        out_specs=[pl.BlockSpec((t, d_model), lambda i: (i, 0))] * 4
        + [pl.BlockSpec((nh, 1, LANES, t), lambda i: (0, i, 0, 0))],
        out_shape=[jax.ShapeDtypeStruct((m, d_model), BF16),
                   jax.ShapeDtypeStruct((m, d_model), F32),
                   jax.ShapeDtypeStruct((m, d_model), BF16),
                   jax.ShapeDtypeStruct((m, d_model), BF16),
                   jax.ShapeDtypeStruct((nh, m // t, LANES, t), BF16)],
        compiler_params=_cp(),
        name='rope',
    )(qkv, qkv, cos, sin)


def _diff_lambda(lam_ref, lam_init):
    lp = lam_ref[...]
    return (jnp.exp(jnp.sum(lp[0:1] * lp[1:2], axis=-1, keepdims=True))
            - jnp.exp(jnp.sum(lp[2:3] * lp[3:4], axis=-1, keepdims=True)) + lam_init)


def _attn_prompt_body(q_ref, k_ref, vt_ref, lam_ref, g_ref, o_ref, acc_ref, sa_ref, sb_ref, *, tq, tk, lp, lreal, off,
                      lam_init):
    qi = pl.program_id(1)
    nsub = tq // ATT_TQ_SUB
    chains = [(a, c) for a in range(nsub) for c in range(2)]
    qts, qchs = {}, {}
    for a in range(nsub):
        qt = q_ref[a * ATT_TQ_SUB:(a + 1) * ATT_TQ_SUB, :].astype(F32).T
        feat = lax.broadcasted_iota(jnp.int32, qt.shape, 0)
        qts[a, 0] = jnp.where(feat < DA_HD, qt, 0.0).astype(BF16)
        qts[a, 1] = jnp.where(feat >= DA_HD, qt, 0.0).astype(BF16)
        qpos = qi * tq + a * ATT_TQ_SUB + lax.broadcasted_iota(jnp.int32, (1, ATT_TQ_SUB), 1)
        qchs[a] = (qpos + off) >> 6
    acc_ref[...] = jnp.zeros_like(acc_ref)
    full_last = jnp.minimum(((qi * tq + off) >> 6) * CHUNK - off + CHUNK - 1, lreal - 1)
    any_last = jnp.minimum(((qi * tq + tq - 1 + off) >> 6) * CHUNK - off + CHUNK - 1, lreal - 1)
    nfull = (full_last + 1) // tk
    nk = any_last // tk + 1

    def scores(ki, s_ref):
        kt = k_ref[pl.ds(pl.multiple_of(ki * tk, tk), tk), :]
        for n, ch in enumerate(chains):
            s_ref[n] = jnp.dot(kt, qts[ch], preferred_element_type=F32)

    def consume(ki, s_ref, carry, masked):
        vt = vt_ref[0, ki]
        s = [s_ref[n] for n in range(len(chains))]
        if masked:
            kpos = ki * tk + lax.broadcasted_iota(jnp.int32, (tk, 1), 0)
            for n, (a, _) in enumerate(chains):
                vis = jnp.logical_and(((kpos + off) >> 6) <= qchs[a], kpos < lreal)
                s[n] = jnp.where(vis, s[n], NEG_BIG)
        m_new = [jnp.maximum(carry[2 * n], jnp.max(s[n], axis=0, keepdims=True)) for n in range(len(chains))]
        p = [jnp.exp2(s[n] - m_new[n]) for n in range(len(chains))]
        new = []
        for n in range(len(chains)):
            alpha = jnp.exp2(carry[2 * n] - m_new[n])
            l_new = alpha * carry[2 * n + 1] + jnp.sum(p[n], axis=0, keepdims=True)
            acc_ref[n] = alpha * acc_ref[n] + jnp.dot(vt, p[n].astype(BF16), preferred_element_type=F32)
            new += [m_new[n], l_new]
        return tuple(new)

    npair = nfull // 2
    last_tile = nk - 1

    def pair(j, carry):
        scores(2 * j + 1, sb_ref)
        carry = consume(2 * j, sa_ref, carry, masked=False)
        scores(jnp.minimum(2 * j + 2, last_tile), sa_ref)
        return consume(2 * j + 1, sb_ref, carry, masked=False)

    def single(ki, carry):
        scores(ki, sb_ref)
        return consume(ki, sb_ref, carry, masked=True)

    carry = (jnp.full((1, ATT_TQ_SUB), NEG_BIG, F32), jnp.zeros((1, ATT_TQ_SUB), F32)) * len(chains)
    scores(0, sa_ref)
    carry = lax.fori_loop(0, npair, pair, carry)
    carry = lax.fori_loop(2 * npair, nk, single, carry)
    lam = _diff_lambda(lam_ref, lam_init)
    for a in range(nsub):
        l0, l1 = carry[4 * a + 1], carry[4 * a + 3]
        o = acc_ref[2 * a] / l0 - lam * (acc_ref[2 * a + 1] / l1)
        ms = jnp.mean(o * o, axis=0, keepdims=True)
        o = o * lax.rsqrt(ms + SUBLN_EPS) * g_ref[...] * (1.0 - lam_init)
        o_ref[a * ATT_TQ_SUB:(a + 1) * ATT_TQ_SUB, :] = o.T.astype(o_ref.dtype)


def _attn_sample_body(q_ref, kn_ref, vn_ref, kc_ref, vc_ref, lam_ref, g_ref, o_alias_ref, o_ref, acc_ref, *,
                      tk, past, dec_seq, lam_init):
    del o_alias_ref
    nq = -(-dec_seq // 16) * 16
    q = q_ref[0:nq, :]
    lane = lax.broadcasted_iota(jnp.int32, q.shape, 1)
    zero = jnp.zeros_like(q)
    q2 = jnp.concatenate([jnp.where(lane < DA_HD, q, zero), jnp.where(lane >= DA_HD, q, zero)], axis=0)
    acc_ref[...] = jnp.zeros_like(acc_ref)
    qrow = lax.broadcasted_iota(jnp.int32, (2 * nq, 1), 0)
    qch = (past + jnp.where(qrow >= nq, qrow - nq, qrow)) >> 6

    def tile(kt, vt, vis, carry):
        m_old, l_old = carry
        s = lax.dot_general(q2, kt, _NT, preferred_element_type=F32)
        s = jnp.where(vis, s, NEG_BIG)
        m_new = jnp.maximum(m_old, jnp.max(s, axis=-1, keepdims=True))
        p = jnp.exp2(s - m_new)
        alpha = jnp.exp2(m_old - m_new)
        l_new = alpha * l_old + jnp.sum(p, axis=-1, keepdims=True)
        acc_ref[...] = alpha * acc_ref[...] + jnp.dot(p.astype(BF16), vt, preferred_element_type=F32)
        return m_new, l_new

    def body(ki, c):
        ks = pl.multiple_of(ki * tk, tk)
        kt = kc_ref[0, pl.ds(ks, tk), :].astype(BF16)
        vt = vc_ref[0, pl.ds(ks, tk), :].astype(BF16)
        kpos = ks + lax.broadcasted_iota(jnp.int32, (1, tk), 1)
        return tile(kt, vt, (kpos >> 6) <= qch, c)

    carry = (jnp.full((2 * nq, 1), NEG_BIG, F32), jnp.zeros((2 * nq, 1), F32))
    carry = lax.fori_loop(0, past // tk, body, carry)
    kidx = lax.broadcasted_iota(jnp.int32, (1, nq), 1)
    vis = jnp.logical_and(((past + kidx) >> 6) <= qch, kidx < dec_seq)
    _, l = tile(kn_ref[0:nq, :], vn_ref[0:nq, :], vis, carry)
    lam = _diff_lambda(lam_ref, lam_init)
    on = acc_ref[...] / l
    o = on[:nq] - lam * on[nq:]
    ms = jnp.mean(o * o, axis=-1, keepdims=True)
    o_ref[...] = jnp.zeros_like(o_ref)
    o_ref[0:nq, :] = (o * lax.rsqrt(ms + SUBLN_EPS) * g_ref[...] * (1.0 - lam_init)).astype(o_ref.dtype)


def _attention(q16, k16, v16, vt16, k_cache, v_cache, lam_p, subln_g, lam_init, lay):
    m, d = q16.shape
    nh = d // LANES
    lp = lay['lp']
    ns = lay['ns']
    past = k_cache.shape[1]
    g = subln_g.astype(F32)
    lam_p = lam_p.astype(F32)
    nkt = vt16.shape[1]
    o_prompt = pl.pallas_call(
        functools.partial(_attn_prompt_body, tq=ATT_TQ, tk=ATT_TK, lp=lp, lreal=lay['lreal'], off=CHUNK - N_META,
                          lam_init=lam_init),
        grid=(nh, lp // ATT_TQ),
        in_specs=[pl.BlockSpec((ATT_TQ, LANES), lambda h, i: (i, h)),
                  pl.BlockSpec((lp, LANES), lambda h, i: (0, h)),
                  pl.BlockSpec((1, nkt, LANES, ATT_TK), lambda h, i: (h, 0, 0, 0)),
                  pl.BlockSpec((4, DA_HD), lambda h, i: (0, 0)),
                  pl.BlockSpec((LANES, 1), lambda h, i: (0, 0))],
        out_specs=pl.BlockSpec((ATT_TQ, LANES), lambda h, i: (i, h)),
        out_shape=jax.ShapeDtypeStruct((m, d), BF16),
        scratch_shapes=[pltpu.VMEM((2 * ATT_TQ // ATT_TQ_SUB, LANES, ATT_TQ_SUB), F32),
                        pltpu.VMEM((2 * ATT_TQ // ATT_TQ_SUB, ATT_TK, ATT_TQ_SUB), F32),
                        pltpu.VMEM((2 * ATT_TQ // ATT_TQ_SUB, ATT_TK, ATT_TQ_SUB), F32)],
        compiler_params=_cp(),
        name='attn_prompt',
    )(q16, k16, vt16, lam_p, g.reshape(LANES, 1))
    nbp = lay['nbp']
    tk = min(512, past)
    nq = -(-lay['dec_seq'] // 16) * 16
    blk_spec = pl.BlockSpec((BLK, LANES), lambda s, h: (nbp + s, h))
    return pl.pallas_call(
        functools.partial(_attn_sample_body, tk=tk, past=past, dec_seq=lay['dec_seq'], lam_init=lam_init),
        grid=(ns, nh),
        in_specs=[blk_spec, blk_spec, blk_spec,
                  pl.BlockSpec((1, past, LANES), lambda s, h: (s, 0, h)),
                  pl.BlockSpec((1, past, LANES), lambda s, h: (s, 0, h)),
                  pl.BlockSpec((4, DA_HD), lambda s, h: (0, 0)),
                  pl.BlockSpec((1, LANES), lambda s, h: (0, 0)),
                  pl.BlockSpec(memory_space=pl.ANY)],
        out_specs=blk_spec,
        out_shape=jax.ShapeDtypeStruct((m, d), BF16),
        scratch_shapes=[pltpu.VMEM((2 * nq, LANES), F32)],
        input_output_aliases={7: 0},
        compiler_params=_cp(),
        name='attn_sample',
    )(q16, k16, v16, k_cache, v_cache, lam_p, g.reshape(1, LANES), o_prompt)


def _shift_mix_body(x_ref, tail_ref, mu_ref, *rest, nbp, lreal, dec_seq):
    outs, win_ref = rest[:-1], rest[-1]
    q = BLK
    b = pl.program_id(0)
    is_start, _ = _block_info(b, nbp, lreal, dec_seq)

    @pl.when(is_start)
    def _():
        win_ref[0:SUBLANES, :] = tail_ref[0]

    x = x_ref[...]
    win_ref[SUBLANES:SUBLANES + q, :] = x
    xx = win_ref[SUBLANES - 1:SUBLANES - 1 + q, :] - x
    for n, o_ref in enumerate(outs):
        o_ref[...] = (x + xx * mu_ref[n:n + 1, :]).astype(o_ref.dtype)
    win_ref[0:SUBLANES, :] = win_ref[q:q + SUBLANES, :]


def _shift_mix(x, tails, mu, lay):
    m, d = x.shape
    n = mu.shape[0]
    body = functools.partial(_shift_mix_body, nbp=lay['nbp'], lreal=lay['lreal'], dec_seq=lay['dec_seq'])
    return pl.pallas_call(
        body,
        grid=(m // BLK,),
        in_specs=[pl.BlockSpec((BLK, d), lambda b: (b, 0)),
                  pl.BlockSpec((1, SUBLANES, d), _stream_map(lay['nbp'], 3)),
                  pl.BlockSpec((n, d), lambda b: (0, 0))],
        out_specs=[pl.BlockSpec((BLK, d), lambda b: (b, 0))] * n,
        out_shape=[jax.ShapeDtypeStruct((m, d), BF16)] * n,
        scratch_shapes=[pltpu.VMEM((SUBLANES + BLK, d), F32)],
        compiler_params=_cp(),
        name='shift_mix',
    )(x, tails, mu)


def _half_sum(x, low):
    s_lo = jnp.sum(jnp.where(low, x, 0.0), axis=-1, keepdims=True)
    s_hi = jnp.sum(jnp.where(low, 0.0, x), axis=-1, keepdims=True)
    return jnp.where(low, s_lo, s_hi)


def _wkv_body(r_ref, k_ref, v_ref, lw_ref, la_ref, g_ref, s0_ref,
              w0_ref, a0_ref, kk_ref, ka_ref, lng_ref, lnb_ref, rk_ref,
              o_ref, st_ref, stage_ref, yb_ref, pw_ref, x_ref, ak_ref, dd_ref, es_ref, bke_ref, gc_ref,
              *, nbp, lreal, dec_seq):
    c = BLK
    npair = st_ref.shape[1]
    half = c // 2
    b = pl.program_id(0)
    is_start, nvalid = _block_info(b, nbp, lreal, dec_seq)

    @pl.when(is_start)
    def _():
        st_ref[0] = s0_ref[0]

    for i, ref in enumerate((r_ref, k_ref, v_ref, lw_ref, la_ref, g_ref)):
        for p in range(npair):
            stage_ref[i, p] = ref[:, p * LANES:(p + 1) * LANES]

    row = lax.broadcasted_iota(jnp.int32, (c, 1), 0)
    valid = row < nvalid
    lane = lax.broadcasted_iota(jnp.int32, (c, LANES), 1)
    low = lane < RK_HD
    ri = lax.broadcasted_iota(jnp.int32, (c, c), 0)
    ci = lax.broadcasted_iota(jnp.int32, (c, c), 1)
    lower_incl = ri >= ci
    lower_strict = ri > ci
    tri = lower_incl.astype(F32)
    eye = (ri == ci).astype(F32)
    same_head = (ri < RK_HD) == (ci < RK_HD)

    def prep(p, carry):
        r = stage_ref[0, p]
        k = stage_ref[1, p]
        v = jnp.where(valid, stage_ref[2, p], 0.0)
        lw = stage_ref[3, p]
        la = stage_ref[4, p]
        w_log = -_softplus(-(w0_ref[p] + lw)) - 0.5
        logw = jnp.where(valid, -jnp.exp(w_log), 0.0)
        ag = _sigmoid(a0_ref[p] + la)
        kk = k * kk_ref[p]
        kk = kk / jnp.maximum(jnp.sqrt(_half_sum(kk * kk, low)), 1e-12)
        kx = k * (1.0 + (ag - 1.0) * ka_ref[p])
        a = jnp.where(valid, -kk, 0.0)
        bv = jnp.where(valid, kk * ag, 0.0)
        kxm = jnp.where(valid, kx, 0.0)

        cum = jnp.dot(tri, logw, preferred_element_type=F32, precision=lax.Precision.HIGHEST)
        cprev = cum - logw
        cmid = cum[half - 1:half, :]
        cend = cum[c - 1:c, :]
        a_abs = a * jnp.exp(cprev)
        r_abs = r * jnp.exp(cum)
        a_rel = a * jnp.exp(cprev - cmid)
        r_rel = r * jnp.exp(cum - cmid)
        e_inv = jnp.exp(cmid - cum)
        b_rel = bv * e_inv
        k_rel = kxm * e_inv
        e_end = jnp.exp(cend - cum)
        bk_end = jnp.concatenate([bv * e_end, kxm * e_end], axis=0)

        stage_ref[1, p] = kxm
        stage_ref[2, p] = v
        s = st_ref[0, p]
        es_ref[p] = _bdot_nt(jnp.concatenate([a_abs, r_abs], axis=0), s)
        bke_ref[p] = bk_end.astype(BF16)
        gc_ref[p] = jnp.broadcast_to(jnp.exp(cend), (SUBLANES, LANES))
        bk_rel = jnp.concatenate([b_rel, k_rel], axis=0).astype(BF16)
        for j, sel in enumerate((low, jnp.logical_not(low))):
            ar = jnp.concatenate([jnp.where(sel, a_rel, 0.0), jnp.where(sel, r_rel, 0.0)], axis=0)
            mm = lax.dot_general(ar.astype(BF16), bk_rel, _NT, preferred_element_type=F32)
            n = jnp.where(lower_strict, mm[:c, :c], 0.0)
            h = 2 * p + j
            pw_ref[h] = n.astype(BF16)
            x_ref[h] = eye + n
            ak_ref[h] = jnp.where(lower_strict, mm[:c, c:], 0.0).astype(BF16)
            dd_ref[h] = jnp.concatenate([jnp.where(lower_incl, mm[c:, :c], 0.0),
                                         jnp.where(lower_incl, mm[c:, c:], 0.0)], axis=1).astype(BF16)
        return carry

    lax.fori_loop(0, npair, prep, 0)

    nlev = int(math.log2(c))
    inv_group = 8

    def invert(g, carry):
        hs = [g * inv_group + j for j in range(inv_group)]
        for h in hs:
            pw = pw_ref[h]
            pw_ref[h] = jnp.dot(pw, pw, preferred_element_type=F32).astype(BF16)
        for _ in range(1, nlev - 1):
            for h in hs:
                pw = pw_ref[h]
                both = jnp.dot(jnp.concatenate([pw, x_ref[h].astype(BF16)], axis=0), pw,
                               preferred_element_type=F32)
                pw_ref[h] = both[:c].astype(BF16)
                x_ref[h] = x_ref[h] + both[c:]
        for h in hs:
            x = x_ref[h]
            x_ref[h] = x + jnp.dot(x.astype(BF16), pw_ref[h], preferred_element_type=F32)
        return carry

    lax.fori_loop(0, 2 * npair // inv_group, invert, 0)

    out_group = 4

    def emit(g, carry):
        ps = [g * out_group + i for i in range(out_group)]
        hs = [(i, j) for i in range(out_group) for j in range(2)]
        rhs, us, ys = {}, {}, {}
        for i, j in hs:
            p = ps[i]
            rhs[i, j] = es_ref[p, 0:c, :] + jnp.dot(ak_ref[2 * p + j], stage_ref[2, p].astype(BF16),
                                                    preferred_element_type=F32)
        for i, j in hs:
            us[i, j] = _bdot(x_ref[2 * ps[i] + j], rhs[i, j])
        for i, j in hs:
            p = ps[i]
            uv = jnp.concatenate([us[i, j], stage_ref[2, p]], axis=0).astype(BF16)
            ys[i, j] = es_ref[p, c:2 * c, :] + jnp.dot(dd_ref[2 * p + j], uv, preferred_element_type=F32)
        for i, p in enumerate(ps):
            v = stage_ref[2, p]
            u = jnp.where(low, us[i, 0], us[i, 1])
            y = jnp.where(low, ys[i, 0], ys[i, 1])
            uv_t = jnp.concatenate([u, v], axis=0).T.astype(BF16)
            upd = jnp.dot(uv_t, bke_ref[p], preferred_element_type=F32)
            st_ref[0, p] = st_ref[0, p] * gc_ref[p, 0:1, :] + jnp.where(same_head, upd, 0.0)

            mean = _half_sum(y, low) * (1.0 / RK_HD)
            dy = y - mean
            var = _half_sum(dy * dy, low) * (1.0 / RK_HD)
            yn = dy * lax.rsqrt(var + RK_GN_EPS) * lng_ref[p] + lnb_ref[p]
            bonus = _half_sum(stage_ref[0, p] * stage_ref[1, p] * rk_ref[p], low) * v
            yb_ref[p] = ((yn + bonus) * stage_ref[5, p]).astype(yb_ref.dtype)
        return carry

    lax.fori_loop(0, npair // out_group, emit, 0)
    for p in range(npair):
        o_ref[:, p * LANES:(p + 1) * LANES] = yb_ref[p]


def _wkv(r, k, v, lw, la, gate, s0, w0, a0, k_k, k_a, lnx_g, lnx_b, r_k, lay):
    m, d = r.shape
    npair = d // LANES

    def prm(t):
        return t.astype(F32).reshape(npair, 1, LANES)

    body = functools.partial(_wkv_body, nbp=lay['nbp'], lreal=lay['lreal'], dec_seq=lay['dec_seq'])
    row_spec = pl.BlockSpec((BLK, d), lambda b: (b, 0))
    prm_spec = pl.BlockSpec((npair, 1, LANES), lambda b: (0, 0, 0))
    smap = _stream_map(lay['nbp'], 4)
    return pl.pallas_call(
        body,
        grid=(m // BLK,),
        in_specs=[row_spec] * 6 + [pl.BlockSpec((1,) + s0.shape[1:], smap)] + [prm_spec] * 7,
        out_specs=[row_spec, pl.BlockSpec((1,) + s0.shape[1:], smap)],
        out_shape=[jax.ShapeDtypeStruct((m, d), BF16), jax.ShapeDtypeStruct(s0.shape, F32)],
        scratch_shapes=[pltpu.VMEM((6, npair, BLK, LANES), F32),
                        pltpu.VMEM((npair, BLK, LANES), BF16),
                        pltpu.VMEM((2 * npair, BLK, BLK), BF16),
                        pltpu.VMEM((2 * npair, BLK, BLK), F32),
                        pltpu.VMEM((2 * npair, BLK, BLK), BF16),
                        pltpu.VMEM((2 * npair, BLK, 2 * BLK), BF16),
                        pltpu.VMEM((npair, 2 * BLK, LANES), F32),
                        pltpu.VMEM((npair, 2 * BLK, LANES), BF16),
                        pltpu.VMEM((npair, SUBLANES, LANES), F32)],
        compiler_params=_cp(),
        name='wkv7',
    )(r, k, v, lw, la, gate, s0, prm(w0), prm(a0), prm(k_k), prm(k_a), prm(lnx_g), prm(lnx_b), prm(r_k))


def _tails(prev, nstream):
    ns, w, c = prev.shape
    t = jnp.zeros((nstream, SUBLANES, c), F32)
    return t.at[1:, SUBLANES - w:, :].set(prev.astype(F32))


def _last_rows(flat, w, lay):
    lreal, lp, ns, dec = lay['lreal'], lay['lp'], lay['ns'], lay['dec_seq']
    p = flat[lreal - w:lreal][None]
    s = flat[lp:].reshape(ns, BLK, -1)[:, dec - w:dec]
    return p, s


def kernel(x_prompt, x_sample, cache_attn_k, cache_attn_v, state_ssm, state_ssd_conv, state_wkv, state_rwkv_shift, state_ffn_conv, meta_tokens, ssd_w_in, ssd_conv_w, ssd_conv_b, ssd_dt_bias, ssd_a_log, ssd_d, ssd_norm_w, ssd_w_out, da_w_qkv, da_lambda, da_subln_g, da_w_o, rk_mu, rk_w0, rk_w1, rk_w2, rk_a0, rk_a1, rk_a2, rk_g1, rk_g2, rk_k_k, rk_k_a, rk_r_k, rk_w_r, rk_w_k, rk_w_v, rk_w_o, rk_lnx_g, rk_lnx_b, ffn_w_up, ffn_w_gate, ffn_conv_w, ffn_conv_b, ffn_w_down, ln_g, ln_b):
    bp, seq, d = x_prompt.shape
    ns, dec_seq, _ = x_sample.shape
    depth = ln_g.shape[0]
    past = cache_attn_k.shape[2]
    assert bp == 1 and dec_seq >= SSD_CONV - 1 and dec_seq <= BLK and past % min(512, past) == 0
    lreal = N_META + seq
    row_align = max(TM, ATT_TQ, ATT_TK)
    lp = -(-lreal // row_align) * row_align
    m = lp + ns * BLK
    assert m % row_align == 0
    nstream = ns + 1
    lay = dict(lreal=lreal, lp=lp, nbp=lp // BLK, ns=ns, dec_seq=dec_seq)
    alpha = (2 * depth) ** 0.25

    x = jnp.concatenate([
        meta_tokens.astype(F32), x_prompt[0], jnp.zeros((lp - lreal, d), F32),
        jnp.pad(x_sample, ((0, 0), (0, BLK - dec_seq), (0, 0))).reshape(ns * BLK, d)], axis=0)

    pos = jnp.concatenate([jnp.arange(lp), jnp.tile(past + jnp.arange(BLK), ns)]).astype(F32)
    half = DA_HD // 2
    inv = ROPE_THETA ** (-jnp.arange(half, dtype=F32) / half)
    ang = pos[:, None] * inv[None, :]
    cos_t = jnp.tile(jnp.cos(ang), (1, LANES // half))
    sin_t = jnp.tile(jnp.sin(ang), (1, LANES // half))

    inner = ssd_norm_w.shape[1]
    nheads = ssd_dt_bias.shape[1]
    hpg = nheads // SSD_GROUPS
    new = {n: [] for n in ('k', 'v', 'ssm', 'ssd_conv', 'wkv', 'shift', 'ffn_conv')}

    for i in range(depth):
        j, kind = i // N_MIXERS, i % N_MIXERS
        if kind == 0:
            w_in = ssd_w_in[j]
            cdim = ssd_conv_w.shape[2]
            w_z = w_in[:, :inner].astype(BF16)
            w_xbc = w_in[:, inner:inner + cdim].astype(BF16)
            w_dt = jnp.pad(w_in[:, inner + cdim:], ((0, 0), (0, LANES - nheads))).astype(BF16)
            z, dt = _mm_multi(x, [w_z, w_dt], [F32, F32], 'ssd_in_z_dt')
            xc, conv_rows = _mm_conv(x, w_xbc, None, _tails(state_ssd_conv[j], nstream), ssd_conv_w[j],
                                     ssd_conv_b[j], F32, lay, 'ssd_in_xbc_conv')
            s0 = jnp.concatenate([jnp.zeros((1,) + state_ssm.shape[2:], F32), state_ssm[j].astype(F32)], axis=0)
            s0t = s0.reshape(nstream, SSD_GROUPS, hpg * SSD_HEADDIM, SSD_STATE).swapaxes(2, 3)
            y, st = _ssd_scan(xc, z, dt, s0t, ssd_dt_bias[j], ssd_a_log[j], ssd_d[j], ssd_norm_w[j], lay)
            w_o = ssd_w_out[j].astype(BF16)
            conv_rows = conv_rows[:, SUBLANES - (SSD_CONV - 1):]
            new['ssd_conv'].append((conv_rows[:1], conv_rows[1:]))
            ssm = st.swapaxes(2, 3).reshape(nstream, nheads, SSD_HEADDIM, SSD_STATE)
            new['ssm'].append((ssm[:1], ssm[1:]))
        elif kind == 1:
            lam_init = 0.8 - 0.6 * math.exp(-0.3 * i)
            (qkv,) = _mm_multi(x, [da_w_qkv[j].astype(BF16)], [F32], 'da_qkv')
            q16, k32, k16, v16, vt16 = _rope(qkv, cos_t, sin_t, d)
            nh = d // (2 * DA_HD)
            y = _attention(q16, k16, v16, vt16, cache_attn_k[j].reshape(ns, past, d),
                           cache_attn_v[j].reshape(ns, past, d), da_lambda[j], da_subln_g[j], lam_init, lay)
            w_o = da_w_o[j].astype(BF16)
            kp, ks = k32[:lreal][None], k32[lp:].reshape(ns, BLK, d)[:, :dec_seq]
            v32 = qkv[:, 2 * d:]
            vp, vs = v32[:lreal][None], v32[lp:].reshape(ns, BLK, d)[:, :dec_seq]
            new['k'].append((kp.reshape(1, lreal, nh, 2, DA_HD), ks.reshape(ns, dec_seq, nh, 2, DA_HD)))
            new['v'].append((vp.reshape(1, lreal, nh, 2 * DA_HD), vs.reshape(ns, dec_seq, nh, 2 * DA_HD)))
        else:
            xr, xw, xk, xv, xa, xg = _shift_mix(x, _tails(state_rwkv_shift[j], nstream), rk_mu[j], lay)
            (r,) = _mm_multi(xr, [rk_w_r[j].astype(BF16)], [F32], 'rk_r')
            (k,) = _mm_multi(xk, [rk_w_k[j].astype(BF16)], [F32], 'rk_k')
            (v,) = _mm_multi(xv, [rk_w_v[j].astype(BF16)], [F32], 'rk_v')
            lw = _lora(xw, rk_w1[j], rk_w2[j], 'tanh', 'rk_lora_w')
            la = _lora(xa, rk_a1[j], rk_a2[j], 'none', 'rk_lora_a')
            gate = _lora(xg, rk_g1[j], rk_g2[j], 'sigmoid', 'rk_lora_g')
            nrh = d // RK_HD
            npair = nrh // 2
            sw = jnp.concatenate([jnp.zeros((1,) + state_wkv.shape[2:], F32), state_wkv[j].astype(F32)], axis=0)
            sw = sw.reshape(nstream, npair, 2, RK_HD, RK_HD)
            zero = jnp.zeros_like(sw[:, :, 0])
            s0 = jnp.concatenate([jnp.concatenate([sw[:, :, 0], zero], axis=-1),
                                  jnp.concatenate([zero, sw[:, :, 1]], axis=-1)], axis=-2)
            y, st = _wkv(r, k, v, lw, la, gate, s0, rk_w0[j], rk_a0[j], rk_k_k[j], rk_k_a[j], rk_lnx_g[j],
                         rk_lnx_b[j], rk_r_k[j].reshape(-1), lay)
            w_o = rk_w_o[j].astype(BF16)
            wkv = jnp.stack([st[:, :, :RK_HD, :RK_HD], st[:, :, RK_HD:, RK_HD:]], axis=2).reshape(
                nstream, nrh, RK_HD, RK_HD)
            new['wkv'].append((wkv[:1], wkv[1:]))
            new['shift'].append(_last_rows(x, 1, lay))
        x = _mm_ln(y, w_o, x, ln_g[i, 0], ln_b[i, 0], alpha, 'mixer_out_ln')
        h, gate_rows = _mm_conv(x, ffn_w_gate[i].astype(BF16), ffn_w_up[i].astype(BF16),
                                _tails(state_ffn_conv[i], nstream), ffn_conv_w[i], ffn_conv_b[i], BF16, lay, 'ffn_in_conv')
        gate_rows = gate_rows[:, SUBLANES - (FFN_CONV - 1):]
        new['ffn_conv'].append((gate_rows[:1], gate_rows[1:]))
        x = _mm_ln(h, ffn_w_down[i].astype(BF16), x, ln_g[i, 1], ln_b[i, 1], alpha, 'ffn_out_ln')

    y_prompt = x[N_META:lreal][None]
    y_sample = x[lp:].reshape(ns, BLK, d)[:, :dec_seq]
    order = ('k', 'v', 'ssm', 'ssd_conv', 'wkv', 'shift', 'ffn_conv')
    prompt_states = tuple(jnp.stack([t[0] for t in new[n]]) for n in order)
    sample_states = tuple(jnp.stack([t[1] for t in new[n]]) for n in order)
    return (y_prompt, y_sample) + prompt_states + sample_states
```

```python
import functools
import math

import jax
import jax.numpy as jnp
from jax import lax
from jax.experimental import pallas as pl
from jax.experimental.pallas import tpu as pltpu

F32 = jnp.float32
BF16 = jnp.bfloat16

CHUNK = 64
N_META = 16
N_MIXERS = 3
SSD_HEADDIM = 64
SSD_GROUPS = 8
SSD_STATE = 128
SSD_CONV = 4
SSD_NORM_EPS = 1e-5
DA_HD = 64
ROPE_THETA = 10000.0
SUBLN_EPS = 1e-5
RK_HD = 64
RK_GN_EPS = 64e-5
FFN_CONV = 3
LN_EPS = 1e-5

LANES = 128
SUBLANES = 8
BLK = 128
TM = 512
ATT_TQ = 512
ATT_TQ_SUB = 256
ATT_TK = 512
ATT_TK_TAIL = 128
VMEM_LIMIT = 56 * 1024 * 1024
NEG_BIG = -1e30

_NT = (((1,), (1,)), ((), ()))


def _cp():
    return pltpu.CompilerParams(vmem_limit_bytes=VMEM_LIMIT)


def _sigmoid(x):
    return 0.5 + 0.5 * jnp.tanh(0.5 * x)


def _softplus(x):
    return jnp.maximum(x, 0.0) + jnp.log(1.0 + jnp.exp(-jnp.abs(x)))


def _bdot(a, b):
    return jnp.dot(a.astype(BF16), b.astype(BF16), preferred_element_type=F32)


def _bdot_nt(a, b):
    return lax.dot_general(a.astype(BF16), b.astype(BF16), _NT, preferred_element_type=F32)


def _mm_multi_body(*refs, n):
    x = refs[0][...].astype(BF16)
    for i in range(n):
        refs[1 + n + i][...] = jnp.dot(x, refs[1 + i][...], preferred_element_type=F32).astype(refs[1 + n + i].dtype)


def _mm_multi(x, ws, out_dtypes, name):
    m, k = x.shape
    n = len(ws)
    outs = pl.pallas_call(
        functools.partial(_mm_multi_body, n=n),
        grid=(m // TM,),
        in_specs=[pl.BlockSpec((TM, k), lambda i: (i, 0))]
        + [pl.BlockSpec(w.shape, lambda i: (0, 0), pipeline_mode=pl.Buffered(1)) for w in ws],
        out_specs=[pl.BlockSpec((TM, w.shape[1]), lambda i: (i, 0)) for w in ws],
        out_shape=[jax.ShapeDtypeStruct((m, w.shape[1]), dt) for w, dt in zip(ws, out_dtypes)],
        compiler_params=_cp(),
        name=name,
    )(x, *ws)
    return outs


def _mm_ln_body(x_ref, w_ref, r_ref, g_ref, b_ref, o_ref, *, alpha):
    acc = jnp.dot(x_ref[...].astype(BF16), w_ref[...], preferred_element_type=F32)
    h = alpha * r_ref[...] + acc
    mu = jnp.mean(h, axis=-1, keepdims=True)
    d = h - mu
    var = jnp.mean(d * d, axis=-1, keepdims=True)
    o_ref[...] = d * lax.rsqrt(var + LN_EPS) * g_ref[...] + b_ref[...]


def _mm_ln(x, w, resid, g, b, alpha, name):
    m, k = x.shape
    d = w.shape[1]
    return pl.pallas_call(
        functools.partial(_mm_ln_body, alpha=alpha),
        grid=(m // TM,),
        in_specs=[pl.BlockSpec((TM, k), lambda i: (i, 0)),
                  pl.BlockSpec((k, d), lambda i: (0, 0), pipeline_mode=pl.Buffered(1)),
                  pl.BlockSpec((TM, d), lambda i: (i, 0)),
                  pl.BlockSpec((1, d), lambda i: (0, 0)),
                  pl.BlockSpec((1, d), lambda i: (0, 0))],
        out_specs=pl.BlockSpec((TM, d), lambda i: (i, 0)),
        out_shape=jax.ShapeDtypeStruct((m, d), F32),
        compiler_params=_cp(),
        name=name,
    )(x, w, resid, g.reshape(1, d), b.reshape(1, d))


def _lora_body(x_ref, w1_ref, w2_ref, o_ref, *, act):
    h = jnp.dot(x_ref[...], w1_ref[...], preferred_element_type=F32)
    if act == 'tanh':
        h = jnp.tanh(h)
    elif act == 'sigmoid':
        h = _sigmoid(h)
    o_ref[...] = jnp.dot(h.astype(BF16), w2_ref[...], preferred_element_type=F32)


def _lora(x, w1, w2, act, name):
    m, k = x.shape
    r = w1.shape[1]
    rp = -(-r // LANES) * LANES
    w1p = jnp.pad(w1, ((0, 0), (0, rp - r))).astype(BF16)
    w2p = jnp.pad(w2, ((0, rp - r), (0, 0))).astype(BF16)
    d = w2.shape[1]
    return pl.pallas_call(
        functools.partial(_lora_body, act=act),
        grid=(m // TM,),
        in_specs=[pl.BlockSpec((TM, k), lambda i: (i, 0)),
                  pl.BlockSpec((k, rp), lambda i: (0, 0)),
                  pl.BlockSpec((rp, d), lambda i: (0, 0))],
        out_specs=pl.BlockSpec((TM, d), lambda i: (i, 0)),
        out_shape=jax.ShapeDtypeStruct((m, d), F32),
        compiler_params=_cp(),
        name=name,
    )(x, w1p, w2p)


def _block_info(b, nbp, lreal, dec_seq):
    is_start = jnp.logical_or(b == 0, b >= nbp)
    nvalid = jnp.where(b < nbp, jnp.clip(lreal - b * BLK, 0, BLK), dec_seq)
    return is_start, nvalid


def _stream_map(nbp, ndim):
    def index_map(b):
        return (jnp.maximum(b - nbp + 1, 0),) + (0,) * (ndim - 1)
    return index_map


def _expand_heads(colmat, g, lane_group):
    out = colmat[:, 4 * g + 3:4 * g + 4]
    for r in (2, 1, 0):
        out = jnp.where(lane_group == r, colmat[:, 4 * g + r:4 * g + r + 1], out)
    return out


def _ssd_body(xc_ref, z_ref, dt_ref, s0_ref, dtb_ref, alog_ref, d_ref, nw_ref,
              y_ref, st_ref, yb_ref, *, nbp, lreal, dec_seq):
    q = BLK
    inner = z_ref.shape[1]
    gn = SSD_GROUPS * SSD_STATE
    b = pl.program_id(0)
    is_start, nvalid = _block_info(b, nbp, lreal, dec_seq)

    @pl.when(is_start)
    def _():
        st_ref[0] = s0_ref[0]

    row = lax.broadcasted_iota(jnp.int32, (q, 1), 0)
    valid = row < nvalid
    dt = _softplus(dt_ref[...] + dtb_ref[...])
    dt = jnp.where(valid, dt, 0.0)
    a = -jnp.exp(alog_ref[...])
    adt = dt * a
    ri = lax.broadcasted_iota(jnp.int32, (q, q), 0)
    ci = lax.broadcasted_iota(jnp.int32, (q, q), 1)
    causal = ri >= ci
    tri = causal.astype(F32)
    acs = jnp.dot(tri, adt, preferred_element_type=F32, precision=lax.Precision.HIGHEST)
    acs_t = acs.T
    last = acs[q - 1:q, :]
    lane_group = lax.broadcasted_iota(jnp.int32, (1, 4 * SSD_HEADDIM), 1) // SSD_HEADDIM

    ssq = jnp.zeros((q, 1), F32)
    for g in range(SSD_GROUPS):
        bb = xc_ref[:, inner + g * SSD_STATE:inner + (g + 1) * SSD_STATE]
        cc = xc_ref[:, inner + gn + g * SSD_STATE:inner + gn + (g + 1) * SSD_STATE]
        xs = xc_ref[:, g * 256:(g + 1) * 256]
        cbm = jnp.where(causal, _bdot_nt(cc, bb), 0.0)
        dt_x = _expand_heads(dt, g, lane_group)
        acs_x = _expand_heads(acs, g, lane_group)
        last_x = _expand_heads(last, g, lane_group)
        xdt = xs * dt_x
        st = st_ref[0, g]
        y = _bdot(cc, st) * jnp.exp(acs_x)
        for r in range(4):
            h = 4 * g + r
            seg = acs[:, h:h + 1] - acs_t[h:h + 1, :]
            att = cbm * jnp.exp(jnp.minimum(seg, 0.0))
            y = y + _bdot(att, jnp.where(lane_group == r, xdt, 0.0))
        xdtd = xdt * jnp.exp(last_x - acs_x)
        st_ref[0, g] = st * jnp.exp(last_x) + _bdot(bb.T, xdtd)
        y = y + _expand_heads(d_ref[...], g, lane_group) * xs
        zz = z_ref[:, g * 256:(g + 1) * 256]
        y = y * (zz * _sigmoid(zz))
        ssq = ssq + jnp.sum(y * y, axis=-1, keepdims=True)
        yb_ref[:, g * 256:(g + 1) * 256] = y
    scale = lax.rsqrt(ssq / inner + SSD_NORM_EPS)
    y_ref[...] = (yb_ref[...] * scale * nw_ref[...]).astype(y_ref.dtype)


def _ssd_scan(xc, z, dt, s0t, dt_bias, a_log, d_skip, norm_w, lay):
    m, cdim = xc.shape
    inner = z.shape[1]
    nh = dt_bias.shape[0]

    def pad_h(v):
        return jnp.pad(v.astype(F32), (0, LANES - nh)).reshape(1, LANES)

    body = functools.partial(_ssd_body, nbp=lay['nbp'], lreal=lay['lreal'], dec_seq=lay['dec_seq'])
    smap4 = _stream_map(lay['nbp'], 4)
    return pl.pallas_call(
        body,
        grid=(m // BLK,),
        in_specs=[pl.BlockSpec((BLK, cdim), lambda b: (b, 0)),
                  pl.BlockSpec((BLK, inner), lambda b: (b, 0)),
                  pl.BlockSpec((BLK, LANES), lambda b: (b, 0)),
                  pl.BlockSpec((1,) + s0t.shape[1:], smap4),
                  pl.BlockSpec((1, LANES), lambda b: (0, 0)),
                  pl.BlockSpec((1, LANES), lambda b: (0, 0)),
                  pl.BlockSpec((1, LANES), lambda b: (0, 0)),
                  pl.BlockSpec((1, inner), lambda b: (0, 0))],
        out_specs=[pl.BlockSpec((BLK, inner), lambda b: (b, 0)),
                   pl.BlockSpec((1,) + s0t.shape[1:], smap4)],
        out_shape=[jax.ShapeDtypeStruct((m, inner), BF16),
                   jax.ShapeDtypeStruct(s0t.shape, F32)],
        scratch_shapes=[pltpu.VMEM((BLK, inner), F32)],
        compiler_params=_cp(),
        name='ssd_scan',
    )(xc, z, dt, s0t, pad_h(dt_bias), pad_h(a_log), pad_h(d_skip), norm_w.reshape(1, inner))


def _conv_taps(a, prev8, cw_ref, cb, cs, width):
    groups = [prev8] + [a[g * SUBLANES:(g + 1) * SUBLANES] for g in range(a.shape[0] // SUBLANES)]
    first_row = lax.broadcasted_iota(jnp.int32, (SUBLANES, 1), 0) == 0
    w0 = cw_ref[0:1, cs]
    z = [x * w0 for x in groups]
    for j in range(1, width):
        wj = cw_ref[j:j + 1, cs]
        rot = [pltpu.roll(t, 1, 0) for t in z]
        z = [groups[0] * wj + rot[0]] + [groups[g] * wj + jnp.where(first_row, rot[g - 1], rot[g])
                                         for g in range(1, len(groups))]
    return jnp.concatenate(z[1:], axis=0) + cb


def _mm_conv_body(*refs, width, gated, nbp, lreal, dec_seq, tm, chunk):
    if gated:
        x_ref, w_ref, wu_ref, tail_ref, cw_ref, cb_ref, o_ref, st_ref, carry_ref = refs
    else:
        x_ref, w_ref, tail_ref, cw_ref, cb_ref, o_ref, st_ref, carry_ref = refs
    i = pl.program_id(0)
    nsub = tm // BLK
    n = w_ref.shape[1]
    nv_last = lreal - ((lreal - 1) // BLK) * BLK
    x = x_ref[...].astype(BF16)
    row = lax.broadcasted_iota(jnp.int32, (BLK, 1), 0)

    @pl.when(i == 0)
    def _():
        st_ref[...] = jnp.zeros_like(st_ref)
        carry_ref[...] = jnp.zeros_like(carry_ref)

    for c0 in range(0, n, chunk):
        cs = slice(c0, c0 + chunk)
        acc = jnp.dot(x, w_ref[:, cs], preferred_element_type=F32)
        if gated:
            up = jnp.dot(x, wu_ref[:, cs], preferred_element_type=F32)
        for j in range(nsub):
            blk = i * nsub + j
            is_start, nvalid = _block_info(blk, nbp, lreal, dec_seq)
            sid = jnp.maximum(blk - nbp + 1, 0)
            rows = slice(j * BLK, (j + 1) * BLK)
            a = acc[rows]
            before = carry_ref[:, cs] if j == 0 else acc[j * BLK - SUBLANES:j * BLK]
            prev8 = jnp.where(is_start, tail_ref[sid, :, cs], before)
            conv = _conv_taps(a, prev8, cw_ref, cb_ref[:, cs], cs, width)
            act = conv * _sigmoid(conv)
            if gated:
                o_ref[rows, cs] = (act * up[rows]).astype(o_ref.dtype)
            else:
                o_ref[rows, cs] = jnp.where(row < nvalid, act, 0.0).astype(o_ref.dtype)
            both = jnp.concatenate([prev8, a], axis=0)
            cand = jnp.where(blk >= nbp, both[dec_seq:dec_seq + SUBLANES],
                             jnp.where(nvalid == BLK, both[BLK:BLK + SUBLANES], both[nv_last:nv_last + SUBLANES]))
            st_ref[sid, :, cs] = jnp.where(nvalid > 0, cand, st_ref[sid, :, cs])
        carry_ref[:, cs] = acc[tm - SUBLANES:tm]


def _mm_conv(x, w, wu, tails, conv_w, conv_b, out_dtype, lay, name):
    m, k = x.shape
    n = w.shape[1]
    width = conv_w.shape[0]
    nstream = tails.shape[0]
    gated = wu is not None
    body = functools.partial(_mm_conv_body, width=width, gated=gated, nbp=lay['nbp'], lreal=lay['lreal'],
                             dec_seq=lay['dec_seq'], tm=TM, chunk=2 * LANES)
    w_spec = pl.BlockSpec((k, n), lambda i: (0, 0), pipeline_mode=pl.Buffered(1))
    full = lambda shape: pl.BlockSpec(shape, lambda i: (0,) * len(shape))
    return pl.pallas_call(
        body,
        grid=(m // TM,),
        in_specs=[pl.BlockSpec((TM, k), lambda i: (i, 0)), w_spec] + ([w_spec] if gated else [])
        + [full((nstream, SUBLANES, n)), full((width, n)), full((1, n))],
        out_specs=[pl.BlockSpec((TM, n), lambda i: (i, 0)), full((nstream, SUBLANES, n))],
        out_shape=[jax.ShapeDtypeStruct((m, n), out_dtype), jax.ShapeDtypeStruct((nstream, SUBLANES, n), F32)],
        scratch_shapes=[pltpu.VMEM((SUBLANES, n), F32)],
        compiler_params=_cp(),
        name=name,
    )(*([x, w] + ([wu] if gated else []) + [tails, conv_w, conv_b.reshape(1, n)]))


def _rope_body(qk_ref, v_ref, cos_ref, sin_ref, q16_ref, k32_ref, k16_ref, v16_ref, vt_ref, *, d_model, scale):
    cos = cos_ref[...]
    sin = sin_ref[...]
    lane = lax.broadcasted_iota(jnp.int32, cos.shape, 1)
    first_half = (lane % DA_HD) < (DA_HD // 2)
    for c in range(2 * d_model // LANES):
        x = qk_ref[:, c * LANES:(c + 1) * LANES]
        partner = jnp.where(first_half, -pltpu.roll(x, LANES - DA_HD // 2, 1), pltpu.roll(x, DA_HD // 2, 1))
        out = x * cos + partner * sin
        if c < d_model // LANES:
            q16_ref[:, c * LANES:(c + 1) * LANES] = (out * scale).astype(BF16)
        else:
            c2 = c - d_model // LANES
            k32_ref[:, c2 * LANES:(c2 + 1) * LANES] = out
            k16_ref[:, c2 * LANES:(c2 + 1) * LANES] = out.astype(BF16)
    v = v_ref[...]
    v16_ref[...] = v.astype(BF16)
    for h in range(d_model // LANES):
        vt_ref[h, 0] = v[:, h * LANES:(h + 1) * LANES].T.astype(BF16)


def _rope(qkv, cos, sin, d_model):
    m = qkv.shape[0]
    t = ATT_TK
    nh = d_model // LANES
    return pl.pallas_call(
        functools.partial(_rope_body, d_model=d_model, scale=DA_HD ** -0.5 * math.log2(math.e)),
        grid=(m // t,),
        in_specs=[pl.BlockSpec((t, 2 * d_model), lambda i: (i, 0)),
                  pl.BlockSpec((t, d_model), lambda i: (i, 2)),
                  pl.BlockSpec((t, LANES), lambda i: (i, 0)),
                  pl.BlockSpec((t, LANES), lambda i: (i, 0))],
        out_specs=[pl.BlockSpec((t, d_model), lambda i: (i, 0))] * 4
        + [pl.BlockSpec((nh, 1, LANES, t), lambda i: (0, i, 0, 0))],
        out_shape=[jax.ShapeDtypeStruct((m, d_model), BF16),
                   jax.ShapeDtypeStruct((m, d_model), F32),
                   jax.ShapeDtypeStruct((m, d_model), BF16),
                   jax.ShapeDtypeStruct((m, d_model), BF16),
                   jax.ShapeDtypeStruct((nh, m // t, LANES, t), BF16)],
        compiler_params=_cp(),
        name='rope',
    )(qkv, qkv, cos, sin)


def _diff_lambda(lam_ref, lam_init):
    lp = lam_ref[...]
    return (jnp.exp(jnp.sum(lp[0:1] * lp[1:2], axis=-1, keepdims=True))
            - jnp.exp(jnp.sum(lp[2:3] * lp[3:4], axis=-1, keepdims=True)) + lam_init)


def _attn_prompt_body(q_ref, k_ref, vt_ref, lam_ref, g_ref, o_ref, acc_ref, sa_ref, sb_ref, *, tq, tk, lp, lreal, off,
                      lam_init):
    qi = pl.program_id(1)
    nsub = tq // ATT_TQ_SUB
    chains = [(a, c) for a in range(nsub) for c in range(2)]
    qts, qchs = {}, {}
    for a in range(nsub):
        qt = q_ref[a * ATT_TQ_SUB:(a + 1) * ATT_TQ_SUB, :].astype(F32).T
        feat = lax.broadcasted_iota(jnp.int32, qt.shape, 0)
        qts[a, 0] = jnp.where(feat < DA_HD, qt, 0.0).astype(BF16)
        qts[a, 1] = jnp.where(feat >= DA_HD, qt, 0.0).astype(BF16)
        qpos = qi * tq + a * ATT_TQ_SUB + lax.broadcasted_iota(jnp.int32, (1, ATT_TQ_SUB), 1)
        qchs[a] = (qpos + off) >> 6
    acc_ref[...] = jnp.zeros_like(acc_ref)
    full_last = jnp.minimum(((qi * tq + off) >> 6) * CHUNK - off + CHUNK - 1, lreal - 1)
    any_last = jnp.minimum(((qi * tq + tq - 1 + off) >> 6) * CHUNK - off + CHUNK - 1, lreal - 1)
    nk = any_last // tk + 1
    nfull = jnp.minimum((full_last + 1) // tk, nk - 1)

    def scores(ki, s_ref, width=tk):
        kt = k_ref[pl.ds(pl.multiple_of(ki * tk, tk), width), :]
        for n, ch in enumerate(chains):
            s_ref[n, 0:width, :] = jnp.dot(kt, qts[ch], preferred_element_type=F32)

    def consume(ki, s_ref, carry, masked, width=tk):
        vt = vt_ref[0, ki, :, 0:width]
        s = [s_ref[n, 0:width, :] for n in range(len(chains))]
        if masked:
            kpos = ki * tk + lax.broadcasted_iota(jnp.int32, (width, 1), 0)
            for n, (a, _) in enumerate(chains):
                vis = jnp.logical_and(((kpos + off) >> 6) <= qchs[a], kpos < lreal)
                s[n] = jnp.where(vis, s[n], NEG_BIG)
        m_new = [jnp.maximum(carry[2 * n], jnp.max(s[n], axis=0, keepdims=True)) for n in range(len(chains))]
        p = [jnp.exp2(s[n] - m_new[n]) for n in range(len(chains))]
        new = []
        for n in range(len(chains)):
            alpha = jnp.exp2(carry[2 * n] - m_new[n])
            l_new = alpha * carry[2 * n + 1] + jnp.sum(p[n], axis=0, keepdims=True)
            acc_ref[n] = alpha * acc_ref[n] + jnp.dot(vt, p[n].astype(BF16), preferred_element_type=F32)
            new += [m_new[n], l_new]
        return tuple(new)

    npair = nfull // 2
    last_tile = nk - 1

    def pair(j, carry):
        scores(2 * j + 1, sb_ref)
        carry = consume(2 * j, sa_ref, carry, masked=False)
        scores(jnp.minimum(2 * j + 2, last_tile), sa_ref)
        return consume(2 * j + 1, sb_ref, carry, masked=False)

    def single(ki, carry, width=tk):
        scores(ki, sb_ref, width)
        return consume(ki, sb_ref, carry, True, width)

    carry = (jnp.full((1, ATT_TQ_SUB), NEG_BIG, F32), jnp.zeros((1, ATT_TQ_SUB), F32)) * len(chains)
    scores(0, sa_ref)
    carry = lax.fori_loop(0, npair, pair, carry)
    carry = lax.fori_loop(2 * npair, last_tile, single, carry)
    narrow = any_last - last_tile * tk < ATT_TK_TAIL
    carry = lax.cond(narrow, functools.partial(single, last_tile, width=ATT_TK_TAIL),
                     functools.partial(single, last_tile), carry)
    lam = _diff_lambda(lam_ref, lam_init)
    for a in range(nsub):
        l0, l1 = carry[4 * a + 1], carry[4 * a + 3]
        o = acc_ref[2 * a] / l0 - lam * (acc_ref[2 * a + 1] / l1)
        ms = jnp.mean(o * o, axis=0, keepdims=True)
        o = o * lax.rsqrt(ms + SUBLN_EPS) * g_ref[...] * (1.0 - lam_init)
        o_ref[a * ATT_TQ_SUB:(a + 1) * ATT_TQ_SUB, :] = o.T.astype(o_ref.dtype)


def _attn_sample_body(q_ref, kn_ref, vn_ref, kc_ref, vc_ref, lam_ref, g_ref, o_alias_ref, o_ref, acc_ref, *,
                      tk, past, dec_seq, lam_init):
    del o_alias_ref
    nq = -(-dec_seq // 16) * 16
    q = q_ref[0:nq, :]
    lane = lax.broadcasted_iota(jnp.int32, q.shape, 1)
    zero = jnp.zeros_like(q)
    q2 = jnp.concatenate([jnp.where(lane < DA_HD, q, zero), jnp.where(lane >= DA_HD, q, zero)], axis=0)
    acc_ref[...] = jnp.zeros_like(acc_ref)
    qrow = lax.broadcasted_iota(jnp.int32, (2 * nq, 1), 0)
    qch = (past + jnp.where(qrow >= nq, qrow - nq, qrow)) >> 6

    def tile(kt, vt, vis, carry):
        m_old, l_old = carry
        s = lax.dot_general(q2, kt, _NT, preferred_element_type=F32)
        s = jnp.where(vis, s, NEG_BIG)
        m_new = jnp.maximum(m_old, jnp.max(s, axis=-1, keepdims=True))
        p = jnp.exp2(s - m_new)
        alpha = jnp.exp2(m_old - m_new)
        l_new = alpha * l_old + jnp.sum(p, axis=-1, keepdims=True)
        acc_ref[...] = alpha * acc_ref[...] + jnp.dot(p.astype(BF16), vt, preferred_element_type=F32)
        return m_new, l_new

    def body(ki, c):
        ks = pl.multiple_of(ki * tk, tk)
        kt = kc_ref[0, pl.ds(ks, tk), :].astype(BF16)
        vt = vc_ref[0, pl.ds(ks, tk), :].astype(BF16)
        kpos = ks + lax.broadcasted_iota(jnp.int32, (1, tk), 1)
        return tile(kt, vt, (kpos >> 6) <= qch, c)

    carry = (jnp.full((2 * nq, 1), NEG_BIG, F32), jnp.zeros((2 * nq, 1), F32))
    carry = lax.fori_loop(0, past // tk, body, carry)
    kidx = lax.broadcasted_iota(jnp.int32, (1, nq), 1)
    vis = jnp.logical_and(((past + kidx) >> 6) <= qch, kidx < dec_seq)
    _, l = tile(kn_ref[0:nq, :], vn_ref[0:nq, :], vis, carry)
    lam = _diff_lambda(lam_ref, lam_init)
    on = acc_ref[...] / l
    o = on[:nq] - lam * on[nq:]
    ms = jnp.mean(o * o, axis=-1, keepdims=True)
    o_ref[...] = jnp.zeros_like(o_ref)
    o_ref[0:nq, :] = (o * lax.rsqrt(ms + SUBLN_EPS) * g_ref[...] * (1.0 - lam_init)).astype(o_ref.dtype)


def _attention(q16, k16, v16, vt16, k_cache, v_cache, lam_p, subln_g, lam_init, lay):
    m, d = q16.shape
    nh = d // LANES
    lp = lay['lp']
    ns = lay['ns']
    past = k_cache.shape[1]
    g = subln_g.astype(F32)
    lam_p = lam_p.astype(F32)
    nkt = vt16.shape[1]
    o_prompt = pl.pallas_call(
        functools.partial(_attn_prompt_body, tq=ATT_TQ, tk=ATT_TK, lp=lp, lreal=lay['lreal'], off=CHUNK - N_META,
                          lam_init=lam_init),
        grid=(nh, lp // ATT_TQ),
        in_specs=[pl.BlockSpec((ATT_TQ, LANES), lambda h, i: (i, h)),
                  pl.BlockSpec((lp, LANES), lambda h, i: (0, h)),
                  pl.BlockSpec((1, nkt, LANES, ATT_TK), lambda h, i: (h, 0, 0, 0)),
                  pl.BlockSpec((4, DA_HD), lambda h, i: (0, 0)),
                  pl.BlockSpec((LANES, 1), lambda h, i: (0, 0))],
        out_specs=pl.BlockSpec((ATT_TQ, LANES), lambda h, i: (i, h)),
        out_shape=jax.ShapeDtypeStruct((m, d), BF16),
        scratch_shapes=[pltpu.VMEM((2 * ATT_TQ // ATT_TQ_SUB, LANES, ATT_TQ_SUB), F32),
                        pltpu.VMEM((2 * ATT_TQ // ATT_TQ_SUB, ATT_TK, ATT_TQ_SUB), F32),
                        pltpu.VMEM((2 * ATT_TQ // ATT_TQ_SUB, ATT_TK, ATT_TQ_SUB), F32)],
        compiler_params=_cp(),
        name='attn_prompt',
    )(q16, k16, vt16, lam_p, g.reshape(LANES, 1))
    nbp = lay['nbp']
    tk = min(512, past)
    nq = -(-lay['dec_seq'] // 16) * 16
    blk_spec = pl.BlockSpec((BLK, LANES), lambda s, h: (nbp + s, h))
    return pl.pallas_call(
        functools.partial(_attn_sample_body, tk=tk, past=past, dec_seq=lay['dec_seq'], lam_init=lam_init),
        grid=(ns, nh),
        in_specs=[blk_spec, blk_spec, blk_spec,
                  pl.BlockSpec((1, past, LANES), lambda s, h: (s, 0, h)),
                  pl.BlockSpec((1, past, LANES), lambda s, h: (s, 0, h)),
                  pl.BlockSpec((4, DA_HD), lambda s, h: (0, 0)),
                  pl.BlockSpec((1, LANES), lambda s, h: (0, 0)),
                  pl.BlockSpec(memory_space=pl.ANY)],
        out_specs=blk_spec,
        out_shape=jax.ShapeDtypeStruct((m, d), BF16),
        scratch_shapes=[pltpu.VMEM((2 * nq, LANES), F32)],
        input_output_aliases={7: 0},
        compiler_params=_cp(),
        name='attn_sample',
    )(q16, k16, v16, k_cache, v_cache, lam_p, g.reshape(1, LANES), o_prompt)


def _shift_mix_body(x_ref, tail_ref, mu_ref, *rest, nbp, lreal, dec_seq):
    outs, win_ref = rest[:-1], rest[-1]
    q = BLK
    b = pl.program_id(0)
    is_start, _ = _block_info(b, nbp, lreal, dec_seq)

    @pl.when(is_start)
    def _():
        win_ref[0:SUBLANES, :] = tail_ref[0]

    x = x_ref[...]
    win_ref[SUBLANES:SUBLANES + q, :] = x
    xx = win_ref[SUBLANES - 1:SUBLANES - 1 + q, :] - x
    for n, o_ref in enumerate(outs):
        o_ref[...] = (x + xx * mu_ref[n:n + 1, :]).astype(o_ref.dtype)
    win_ref[0:SUBLANES, :] = win_ref[q:q + SUBLANES, :]


def _shift_mix(x, tails, mu, lay):
    m, d = x.shape
    n = mu.shape[0]
    body = functools.partial(_shift_mix_body, nbp=lay['nbp'], lreal=lay['lreal'], dec_seq=lay['dec_seq'])
    return pl.pallas_call(
        body,
        grid=(m // BLK,),
        in_specs=[pl.BlockSpec((BLK, d), lambda b: (b, 0)),
                  pl.BlockSpec((1, SUBLANES, d), _stream_map(lay['nbp'], 3)),
                  pl.BlockSpec((n, d), lambda b: (0, 0))],
        out_specs=[pl.BlockSpec((BLK, d), lambda b: (b, 0))] * n,
        out_shape=[jax.ShapeDtypeStruct((m, d), BF16)] * n,
        scratch_shapes=[pltpu.VMEM((SUBLANES + BLK, d), F32)],
        compiler_params=_cp(),
        name='shift_mix',
    )(x, tails, mu)


def _half_sum(x, low):
    s_lo = jnp.sum(jnp.where(low, x, 0.0), axis=-1, keepdims=True)
    s_hi = jnp.sum(jnp.where(low, 0.0, x), axis=-1, keepdims=True)
    return jnp.where(low, s_lo, s_hi)


def _wkv_body(r_ref, k_ref, v_ref, lw_ref, la_ref, g_ref, s0_ref,
              w0_ref, a0_ref, kk_ref, ka_ref, lng_ref, lnb_ref, rk_ref,
              o_ref, st_ref, stage_ref, yb_ref, pw_ref, x_ref, ak_ref, dd_ref, es_ref, bke_ref, gc_ref,
              *, nbp, lreal, dec_seq):
    c = BLK
    npair = st_ref.shape[1]
    half = c // 2
    b = pl.program_id(0)
    is_start, nvalid = _block_info(b, nbp, lreal, dec_seq)

    @pl.when(is_start)
    def _():
        st_ref[0] = s0_ref[0]

    for i, ref in enumerate((r_ref, k_ref, v_ref, lw_ref, la_ref, g_ref)):
        for p in range(npair):
            stage_ref[i, p] = ref[:, p * LANES:(p + 1) * LANES]

    row = lax.broadcasted_iota(jnp.int32, (c, 1), 0)
    valid = row < nvalid
    lane = lax.broadcasted_iota(jnp.int32, (c, LANES), 1)
    low = lane < RK_HD
    ri = lax.broadcasted_iota(jnp.int32, (c, c), 0)
    ci = lax.broadcasted_iota(jnp.int32, (c, c), 1)
    lower_incl = ri >= ci
    lower_strict = ri > ci
    tri = lower_incl.astype(F32)
    eye = (ri == ci).astype(F32)
    same_head = (ri < RK_HD) == (ci < RK_HD)

    prep_group = 4

    def each(f, *cols):
        return [f(*args) for args in zip(*cols)]

    def prep(g, carry):
        ps = [g * prep_group + i for i in range(prep_group)]
        r = [stage_ref[0, p] for p in ps]
        k = [stage_ref[1, p] for p in ps]
        v = [jnp.where(valid, stage_ref[2, p], 0.0) for p in ps]
        w_log = [-_softplus(-(w0_ref[p] + stage_ref[3, p])) - 0.5 for p in ps]
        logw = each(lambda t: jnp.where(valid, -jnp.exp(t), 0.0), w_log)
        ag = [_sigmoid(a0_ref[p] + stage_ref[4, p]) for p in ps]
        kk = [ki * kk_ref[p] for ki, p in zip(k, ps)]
        kk = each(lambda t: t / jnp.maximum(jnp.sqrt(_half_sum(t * t, low)), 1e-12), kk)
        kxm = [jnp.where(valid, ki * (1.0 + (gi - 1.0) * ka_ref[p]), 0.0) for ki, gi, p in zip(k, ag, ps)]
        a = each(lambda t: jnp.where(valid, -t, 0.0), kk)
        bv = each(lambda t, gi: jnp.where(valid, t * gi, 0.0), kk, ag)

        cum = each(lambda t: jnp.dot(tri, t, preferred_element_type=F32, precision=lax.Precision.HIGHEST), logw)
        e_cum = each(jnp.exp, cum)
        e_prev = each(lambda ci_, li: jnp.exp(ci_ - li), cum, logw)
        e_neg = each(lambda ci_: jnp.exp(-ci_), cum)
        mid_dn = each(lambda ci_: jnp.exp(-ci_[half - 1:half, :]), cum)
        mid_up = each(lambda ci_: jnp.exp(ci_[half - 1:half, :]), cum)
        end_up = each(lambda ci_: jnp.exp(ci_[c - 1:c, :]), cum)
        a_abs = each(lambda x_, e: x_ * e, a, e_prev)
        r_abs = each(lambda x_, e: x_ * e, r, e_cum)
        bn = each(lambda x_, e: x_ * e, bv, e_neg)
        kn = each(lambda x_, e: x_ * e, kxm, e_neg)

        for i, p in enumerate(ps):
            stage_ref[1, p] = kxm[i]
            stage_ref[2, p] = v[i]
            bke_ref[p] = jnp.concatenate([bn[i] * end_up[i], kn[i] * end_up[i]], axis=0).astype(BF16)
            gc_ref[p] = jnp.broadcast_to(end_up[i], (SUBLANES, LANES))
        for i, p in enumerate(ps):
            s = st_ref[0, p]
            es_ref[p] = _bdot_nt(jnp.concatenate([a_abs[i], r_abs[i]], axis=0), s)
        for i, p in enumerate(ps):
            a_rel = a_abs[i] * mid_dn[i]
            r_rel = r_abs[i] * mid_dn[i]
            bk_rel = jnp.concatenate([bn[i] * mid_up[i], kn[i] * mid_up[i]], axis=0).astype(BF16)
            for j, sel in enumerate((low, jnp.logical_not(low))):
                ar = jnp.concatenate([jnp.where(sel, a_rel, 0.0), jnp.where(sel, r_rel, 0.0)], axis=0)
                mm = lax.dot_general(ar.astype(BF16), bk_rel, _NT, preferred_element_type=F32)
                n = jnp.where(lower_strict, mm[:c, :c], 0.0)
                h = 2 * p + j
                pw_ref[h] = n.astype(BF16)
                x_ref[h] = eye + n
                ak_ref[h] = jnp.where(lower_strict, mm[:c, c:], 0.0).astype(BF16)
                dd_ref[h] = jnp.concatenate([jnp.where(lower_incl, mm[c:, :c], 0.0),
                                             jnp.where(lower_incl, mm[c:, c:], 0.0)], axis=1).astype(BF16)
        return carry

    lax.fori_loop(0, npair // prep_group, prep, 0)

    nlev = int(math.log2(c))
    inv_group = 8

    def invert(g, carry):
        hs = [g * inv_group + j for j in range(inv_group)]
        for h in hs:
            pw = pw_ref[h]
            pw_ref[h] = jnp.dot(pw, pw, preferred_element_type=F32).astype(BF16)
        for _ in range(1, nlev - 1):
            for h in hs:
                pw = pw_ref[h]
                both = jnp.dot(jnp.concatenate([pw, x_ref[h].astype(BF16)], axis=0), pw,
                               preferred_element_type=F32)
                pw_ref[h] = both[:c].astype(BF16)
                x_ref[h] = x_ref[h] + both[c:]
        for h in hs:
            x = x_ref[h]
            x_ref[h] = x + jnp.dot(x.astype(BF16), pw_ref[h], preferred_element_type=F32)
        return carry

    lax.fori_loop(0, 2 * npair // inv_group, invert, 0)

    out_group = 4

    def emit(g, carry):
        ps = [g * out_group + i for i in range(out_group)]
        hs = [(i, j) for i in range(out_group) for j in range(2)]
        rhs, us, ys = {}, {}, {}
        for i, j in hs:
            p = ps[i]
            rhs[i, j] = es_ref[p, 0:c, :] + jnp.dot(ak_ref[2 * p + j], stage_ref[2, p].astype(BF16),
                                                    preferred_element_type=F32)
        for i, j in hs:
            us[i, j] = _bdot(x_ref[2 * ps[i] + j], rhs[i, j])
        for i, j in hs:
            p = ps[i]
            uv = jnp.concatenate([us[i, j], stage_ref[2, p]], axis=0).astype(BF16)
            ys[i, j] = es_ref[p, c:2 * c, :] + jnp.dot(dd_ref[2 * p + j], uv, preferred_element_type=F32)
        for i, p in enumerate(ps):
            v = stage_ref[2, p]
            u = jnp.where(low, us[i, 0], us[i, 1])
            y = jnp.where(low, ys[i, 0], ys[i, 1])
            uv_t = jnp.concatenate([u, v], axis=0).T.astype(BF16)
            upd = jnp.dot(uv_t, bke_ref[p], preferred_element_type=F32)
            st_ref[0, p] = st_ref[0, p] * gc_ref[p, 0:1, :] + jnp.where(same_head, upd, 0.0)

            mean = _half_sum(y, low) * (1.0 / RK_HD)
            dy = y - mean
            var = _half_sum(dy * dy, low) * (1.0 / RK_HD)
            yn = dy * lax.rsqrt(var + RK_GN_EPS) * lng_ref[p] + lnb_ref[p]
            bonus = _half_sum(stage_ref[0, p] * stage_ref[1, p] * rk_ref[p], low) * v
            yb_ref[p] = ((yn + bonus) * stage_ref[5, p]).astype(yb_ref.dtype)
        return carry

    lax.fori_loop(0, npair // out_group, emit, 0)
    for p in range(npair):
        o_ref[:, p * LANES:(p + 1) * LANES] = yb_ref[p]


def _wkv(r, k, v, lw, la, gate, s0, w0, a0, k_k, k_a, lnx_g, lnx_b, r_k, lay):
    m, d = r.shape
    npair = d // LANES

    def prm(t):
        return t.astype(F32).reshape(npair, 1, LANES)

    body = functools.partial(_wkv_body, nbp=lay['nbp'], lreal=lay['lreal'], dec_seq=lay['dec_seq'])
    row_spec = pl.BlockSpec((BLK, d), lambda b: (b, 0))
    prm_spec = pl.BlockSpec((npair, 1, LANES), lambda b: (0, 0, 0))
    smap = _stream_map(lay['nbp'], 4)
    return pl.pallas_call(
        body,
        grid=(m // BLK,),
        in_specs=[row_spec] * 6 + [pl.BlockSpec((1,) + s0.shape[1:], smap)] + [prm_spec] * 7,
        out_specs=[row_spec, pl.BlockSpec((1,) + s0.shape[1:], smap)],
        out_shape=[jax.ShapeDtypeStruct((m, d), BF16), jax.ShapeDtypeStruct(s0.shape, F32)],
        scratch_shapes=[pltpu.VMEM((6, npair, BLK, LANES), F32),
                        pltpu.VMEM((npair, BLK, LANES), BF16),
                        pltpu.VMEM((2 * npair, BLK, BLK), BF16),
                        pltpu.VMEM((2 * npair, BLK, BLK), F32),
                        pltpu.VMEM((2 * npair, BLK, BLK), BF16),
                        pltpu.VMEM((2 * npair, BLK, 2 * BLK), BF16),
                        pltpu.VMEM((npair, 2 * BLK, LANES), F32),
                        pltpu.VMEM((npair, 2 * BLK, LANES), BF16),
                        pltpu.VMEM((npair, SUBLANES, LANES), F32)],
        compiler_params=_cp(),
        name='wkv7',
    )(r, k, v, lw, la, gate, s0, prm(w0), prm(a0), prm(k_k), prm(k_a), prm(lnx_g), prm(lnx_b), prm(r_k))


def _tails(prev, nstream):
    ns, w, c = prev.shape
    t = jnp.zeros((nstream, SUBLANES, c), F32)
    return t.at[1:, SUBLANES - w:, :].set(prev.astype(F32))


def _last_rows(flat, w, lay):
    lreal, lp, ns, dec = lay['lreal'], lay['lp'], lay['ns'], lay['dec_seq']
    p = flat[lreal - w:lreal][None]
    s = flat[lp:].reshape(ns, BLK, -1)[:, dec - w:dec]
    return p, s


def kernel(x_prompt, x_sample, cache_attn_k, cache_attn_v, state_ssm, state_ssd_conv, state_wkv, state_rwkv_shift, state_ffn_conv, meta_tokens, ssd_w_in, ssd_conv_w, ssd_conv_b, ssd_dt_bias, ssd_a_log, ssd_d, ssd_norm_w, ssd_w_out, da_w_qkv, da_lambda, da_subln_g, da_w_o, rk_mu, rk_w0, rk_w1, rk_w2, rk_a0, rk_a1, rk_a2, rk_g1, rk_g2, rk_k_k, rk_k_a, rk_r_k, rk_w_r, rk_w_k, rk_w_v, rk_w_o, rk_lnx_g, rk_lnx_b, ffn_w_up, ffn_w_gate, ffn_conv_w, ffn_conv_b, ffn_w_down, ln_g, ln_b):
    bp, seq, d = x_prompt.shape
    ns, dec_seq, _ = x_sample.shape
    depth = ln_g.shape[0]
    past = cache_attn_k.shape[2]
    assert bp == 1 and dec_seq >= SSD_CONV - 1 and dec_seq <= BLK and past % min(512, past) == 0
    lreal = N_META + seq
    row_align = max(TM, ATT_TQ, ATT_TK)
    lp = -(-lreal // row_align) * row_align
    m = lp + ns * BLK
    assert m % row_align == 0
    nstream = ns + 1
    lay = dict(lreal=lreal, lp=lp, nbp=lp // BLK, ns=ns, dec_seq=dec_seq)
    alpha = (2 * depth) ** 0.25

    x = jnp.concatenate([
        meta_tokens.astype(F32), x_prompt[0], jnp.zeros((lp - lreal, d), F32),
        jnp.pad(x_sample, ((0, 0), (0, BLK - dec_seq), (0, 0))).reshape(ns * BLK, d)], axis=0)

    pos = jnp.concatenate([jnp.arange(lp), jnp.tile(past + jnp.arange(BLK), ns)]).astype(F32)
    half = DA_HD // 2
    inv = ROPE_THETA ** (-jnp.arange(half, dtype=F32) / half)
    ang = pos[:, None] * inv[None, :]
    cos_t = jnp.tile(jnp.cos(ang), (1, LANES // half))
    sin_t = jnp.tile(jnp.sin(ang), (1, LANES // half))

    inner = ssd_norm_w.shape[1]
    nheads = ssd_dt_bias.shape[1]
    hpg = nheads // SSD_GROUPS
    new = {n: [] for n in ('k', 'v', 'ssm', 'ssd_conv', 'wkv', 'shift', 'ffn_conv')}

    for i in range(depth):
        j, kind = i // N_MIXERS, i % N_MIXERS
        if kind == 0:
            w_in = ssd_w_in[j]
            cdim = ssd_conv_w.shape[2]
            w_z = w_in[:, :inner].astype(BF16)
            w_xbc = w_in[:, inner:inner + cdim].astype(BF16)
            w_dt = jnp.pad(w_in[:, inner + cdim:], ((0, 0), (0, LANES - nheads))).astype(BF16)
            z, dt = _mm_multi(x, [w_z, w_dt], [F32, F32], 'ssd_in_z_dt')
            xc, conv_rows = _mm_conv(x, w_xbc, None, _tails(state_ssd_conv[j], nstream), ssd_conv_w[j],
                                     ssd_conv_b[j], F32, lay, 'ssd_in_xbc_conv')
            s0 = jnp.concatenate([jnp.zeros((1,) + state_ssm.shape[2:], F32), state_ssm[j].astype(F32)], axis=0)
            s0t = s0.reshape(nstream, SSD_GROUPS, hpg * SSD_HEADDIM, SSD_STATE).swapaxes(2, 3)
            y, st = _ssd_scan(xc, z, dt, s0t, ssd_dt_bias[j], ssd_a_log[j], ssd_d[j], ssd_norm_w[j], lay)
            w_o = ssd_w_out[j].astype(BF16)
            conv_rows = conv_rows[:, SUBLANES - (SSD_CONV - 1):]
            new['ssd_conv'].append((conv_rows[:1], conv_rows[1:]))
            ssm = st.swapaxes(2, 3).reshape(nstream, nheads, SSD_HEADDIM, SSD_STATE)
            new['ssm'].append((ssm[:1], ssm[1:]))
        elif kind == 1:
            lam_init = 0.8 - 0.6 * math.exp(-0.3 * i)
            (qkv,) = _mm_multi(x, [da_w_qkv[j].astype(BF16)], [F32], 'da_qkv')
            q16, k32, k16, v16, vt16 = _rope(qkv, cos_t, sin_t, d)
            nh = d // (2 * DA_HD)
            y = _attention(q16, k16, v16, vt16, cache_attn_k[j].reshape(ns, past, d),
                           cache_attn_v[j].reshape(ns, past, d), da_lambda[j], da_subln_g[j], lam_init, lay)
            w_o = da_w_o[j].astype(BF16)
            kp, ks = k32[:lreal][None], k32[lp:].reshape(ns, BLK, d)[:, :dec_seq]
            v32 = qkv[:, 2 * d:]
            vp, vs = v32[:lreal][None], v32[lp:].reshape(ns, BLK, d)[:, :dec_seq]
            new['k'].append((kp.reshape(1, lreal, nh, 2, DA_HD), ks.reshape(ns, dec_seq, nh, 2, DA_HD)))
            new['v'].append((vp.reshape(1, lreal, nh, 2 * DA_HD), vs.reshape(ns, dec_seq, nh, 2 * DA_HD)))
        else:
            xr, xw, xk, xv, xa, xg = _shift_mix(x, _tails(state_rwkv_shift[j], nstream), rk_mu[j], lay)
            (r,) = _mm_multi(xr, [rk_w_r[j].astype(BF16)], [F32], 'rk_r')
            (k,) = _mm_multi(xk, [rk_w_k[j].astype(BF16)], [F32], 'rk_k')
            (v,) = _mm_multi(xv, [rk_w_v[j].astype(BF16)], [F32], 'rk_v')
            lw = _lora(xw, rk_w1[j], rk_w2[j], 'tanh', 'rk_lora_w')
            la = _lora(xa, rk_a1[j], rk_a2[j], 'none', 'rk_lora_a')
            gate = _lora(xg, rk_g1[j], rk_g2[j], 'sigmoid', 'rk_lora_g')
            nrh = d // RK_HD
            npair = nrh // 2
            sw = jnp.concatenate([jnp.zeros((1,) + state_wkv.shape[2:], F32), state_wkv[j].astype(F32)], axis=0)
            sw = sw.reshape(nstream, npair, 2, RK_HD, RK_HD)
            zero = jnp.zeros_like(sw[:, :, 0])
            s0 = jnp.concatenate([jnp.concatenate([sw[:, :, 0], zero], axis=-1),
                                  jnp.concatenate([zero, sw[:, :, 1]], axis=-1)], axis=-2)
            y, st = _wkv(r, k, v, lw, la, gate, s0, rk_w0[j], rk_a0[j], rk_k_k[j], rk_k_a[j], rk_lnx_g[j],
                         rk_lnx_b[j], rk_r_k[j].reshape(-1), lay)
            w_o = rk_w_o[j].astype(BF16)
            wkv = jnp.stack([st[:, :, :RK_HD, :RK_HD], st[:, :, RK_HD:, RK_HD:]], axis=2).reshape(
                nstream, nrh, RK_HD, RK_HD)
            new['wkv'].append((wkv[:1], wkv[1:]))
            new['shift'].append(_last_rows(x, 1, lay))
        x = _mm_ln(y, w_o, x, ln_g[i, 0], ln_b[i, 0], alpha, 'mixer_out_ln')
        h, gate_rows = _mm_conv(x, ffn_w_gate[i].astype(BF16), ffn_w_up[i].astype(BF16),
                                _tails(state_ffn_conv[i], nstream), ffn_conv_w[i], ffn_conv_b[i], BF16, lay, 'ffn_in_conv')
        gate_rows = gate_rows[:, SUBLANES - (FFN_CONV - 1):]
        new['ffn_conv'].append((gate_rows[:1], gate_rows[1:]))
        x = _mm_ln(h, ffn_w_down[i].astype(BF16), x, ln_g[i, 1], ln_b[i, 1], alpha, 'ffn_out_ln')

    y_prompt = x[N_META:lreal][None]
    y_sample = x[lp:].reshape(ns, BLK, d)[:, :dec_seq]
    order = ('k', 'v', 'ssm', 'ssd_conv', 'wkv', 'shift', 'ffn_conv')
    prompt_states = tuple(jnp.stack([t[0] for t in new[n]]) for n in order)
    sample_states = tuple(jnp.stack([t[1] for t in new[n]]) for n in order)
    return (y_prompt, y_sample) + prompt_states + sample_states
```

```python
import functools
import math

import jax
import jax.numpy as jnp
from jax import lax
from jax.experimental import pallas as pl
from jax.experimental.pallas import tpu as pltpu

F32 = jnp.float32
BF16 = jnp.bfloat16

CHUNK = 64
N_META = 16
N_MIXERS = 3
SSD_HEADDIM = 64
SSD_GROUPS = 8
SSD_STATE = 128
SSD_CONV = 4
SSD_NORM_EPS = 1e-5
DA_HD = 64
ROPE_THETA = 10000.0
SUBLN_EPS = 1e-5
RK_HD = 64
RK_GN_EPS = 64e-5
FFN_CONV = 3
LN_EPS = 1e-5

LANES = 128
SUBLANES = 8
BLK = 128
TM = 512
ATT_TQ = 512
ATT_TQ_SUB = 256
ATT_TK = 512
ATT_TK_TAIL = 128
ATT_TK_SAMPLE = 4096
VMEM_LIMIT = 56 * 1024 * 1024
NEG_BIG = -1e30

_NT = (((1,), (1,)), ((), ()))


def _cp():
    return pltpu.CompilerParams(vmem_limit_bytes=VMEM_LIMIT)


def _sigmoid(x):
    return 0.5 + 0.5 * jnp.tanh(0.5 * x)


def _softplus(x):
    return jnp.maximum(x, 0.0) + jnp.log(1.0 + jnp.exp(-jnp.abs(x)))


def _bdot(a, b):
    return jnp.dot(a.astype(BF16), b.astype(BF16), preferred_element_type=F32)


def _bdot_nt(a, b):
    return lax.dot_general(a.astype(BF16), b.astype(BF16), _NT, preferred_element_type=F32)


def _mm_multi_body(*refs, n):
    x = refs[0][...].astype(BF16)
    for i in range(n):
        refs[1 + n + i][...] = jnp.dot(x, refs[1 + i][...], preferred_element_type=F32).astype(refs[1 + n + i].dtype)


def _mm_multi(x, ws, out_dtypes, name):
    m, k = x.shape
    n = len(ws)
    outs = pl.pallas_call(
        functools.partial(_mm_multi_body, n=n),
        grid=(m // TM,),
        in_specs=[pl.BlockSpec((TM, k), lambda i: (i, 0))]
        + [pl.BlockSpec(w.shape, lambda i: (0, 0), pipeline_mode=pl.Buffered(1)) for w in ws],
        out_specs=[pl.BlockSpec((TM, w.shape[1]), lambda i: (i, 0)) for w in ws],
        out_shape=[jax.ShapeDtypeStruct((m, w.shape[1]), dt) for w, dt in zip(ws, out_dtypes)],
        compiler_params=_cp(),
        name=name,
    )(x, *ws)
    return outs


def _mm_ln_body(x_ref, w_ref, r_ref, g_ref, b_ref, o_ref, *, alpha):
    acc = jnp.dot(x_ref[...].astype(BF16), w_ref[...], preferred_element_type=F32)
    h = alpha * r_ref[...] + acc
    mu = jnp.mean(h, axis=-1, keepdims=True)
    d = h - mu
    var = jnp.mean(d * d, axis=-1, keepdims=True)
    o_ref[...] = d * lax.rsqrt(var + LN_EPS) * g_ref[...] + b_ref[...]


def _mm_ln(x, w, resid, g, b, alpha, name):
    m, k = x.shape
    d = w.shape[1]
    return pl.pallas_call(
        functools.partial(_mm_ln_body, alpha=alpha),
        grid=(m // TM,),
        in_specs=[pl.BlockSpec((TM, k), lambda i: (i, 0)),
                  pl.BlockSpec((k, d), lambda i: (0, 0), pipeline_mode=pl.Buffered(1)),
                  pl.BlockSpec((TM, d), lambda i: (i, 0)),
                  pl.BlockSpec((1, d), lambda i: (0, 0)),
                  pl.BlockSpec((1, d), lambda i: (0, 0))],
        out_specs=pl.BlockSpec((TM, d), lambda i: (i, 0)),
        out_shape=jax.ShapeDtypeStruct((m, d), F32),
        compiler_params=_cp(),
        name=name,
    )(x, w, resid, g.reshape(1, d), b.reshape(1, d))


def _block_info(b, nbp, lreal, dec_seq):
    is_start = jnp.logical_or(b == 0, b >= nbp)
    nvalid = jnp.where(b < nbp, jnp.clip(lreal - b * BLK, 0, BLK), dec_seq)
    return is_start, nvalid


def _stream_map(nbp, ndim):
    def index_map(b):
        return (jnp.maximum(b - nbp + 1, 0),) + (0,) * (ndim - 1)
    return index_map


def _expand_heads(colmat, g, lane_group):
    out = colmat[:, 4 * g + 3:4 * g + 4]
    for r in (2, 1, 0):
        out = jnp.where(lane_group == r, colmat[:, 4 * g + r:4 * g + r + 1], out)
    return out


def _ssd_body(xc_ref, z_ref, dt_ref, s0_ref, dtb_ref, alog_ref, d_ref, nw_ref,
              y_ref, st_ref, yb_ref, *, nbp, lreal, dec_seq):
    q = BLK
    inner = z_ref.shape[1]
    gn = SSD_GROUPS * SSD_STATE
    b = pl.program_id(0)
    is_start, nvalid = _block_info(b, nbp, lreal, dec_seq)

    @pl.when(is_start)
    def _():
        st_ref[0] = s0_ref[0]

    row = lax.broadcasted_iota(jnp.int32, (q, 1), 0)
    valid = row < nvalid
    dt = _softplus(dt_ref[...] + dtb_ref[...])
    dt = jnp.where(valid, dt, 0.0)
    a = -jnp.exp(alog_ref[...])
    adt = dt * a
    ri = lax.broadcasted_iota(jnp.int32, (q, q), 0)
    ci = lax.broadcasted_iota(jnp.int32, (q, q), 1)
    causal = ri >= ci
    tri = causal.astype(F32)
    acs = jnp.dot(tri, adt, preferred_element_type=F32, precision=lax.Precision.HIGHEST)
    acs_t = acs.T
    last = acs[q - 1:q, :]
    lane_group = lax.broadcasted_iota(jnp.int32, (1, 4 * SSD_HEADDIM), 1) // SSD_HEADDIM

    ssq = jnp.zeros((q, 1), F32)
    for g in range(SSD_GROUPS):
        bb = xc_ref[:, inner + g * SSD_STATE:inner + (g + 1) * SSD_STATE]
        cc = xc_ref[:, inner + gn + g * SSD_STATE:inner + gn + (g + 1) * SSD_STATE]
        xs = xc_ref[:, g * 256:(g + 1) * 256]
        cbm = jnp.where(causal, _bdot_nt(cc, bb), 0.0)
        dt_x = _expand_heads(dt, g, lane_group)
        acs_x = _expand_heads(acs, g, lane_group)
        last_x = _expand_heads(last, g, lane_group)
        xdt = xs * dt_x
        st = st_ref[0, g]
        y = _bdot(cc, st) * jnp.exp(acs_x)
        for r in range(4):
            h = 4 * g + r
            seg = acs[:, h:h + 1] - acs_t[h:h + 1, :]
            att = cbm * jnp.exp(jnp.minimum(seg, 0.0))
            y = y + _bdot(att, jnp.where(lane_group == r, xdt, 0.0))
        xdtd = xdt * jnp.exp(last_x - acs_x)
        st_ref[0, g] = st * jnp.exp(last_x) + _bdot(bb.T, xdtd)
        y = y + _expand_heads(d_ref[...], g, lane_group) * xs
        zz = z_ref[:, g * 256:(g + 1) * 256]
        y = y * (zz * _sigmoid(zz))
        ssq = ssq + jnp.sum(y * y, axis=-1, keepdims=True)
        yb_ref[:, g * 256:(g + 1) * 256] = y
    scale = lax.rsqrt(ssq / inner + SSD_NORM_EPS)
    y_ref[...] = (yb_ref[...] * scale * nw_ref[...]).astype(y_ref.dtype)


def _ssd_scan(xc, z, dt, s0t, dt_bias, a_log, d_skip, norm_w, lay):
    m, cdim = xc.shape
    inner = z.shape[1]
    nh = dt_bias.shape[0]

    def pad_h(v):
        return jnp.pad(v.astype(F32), (0, LANES - nh)).reshape(1, LANES)

    body = functools.partial(_ssd_body, nbp=lay['nbp'], lreal=lay['lreal'], dec_seq=lay['dec_seq'])
    smap4 = _stream_map(lay['nbp'], 4)
    return pl.pallas_call(
        body,
        grid=(m // BLK,),
        in_specs=[pl.BlockSpec((BLK, cdim), lambda b: (b, 0)),
                  pl.BlockSpec((BLK, inner), lambda b: (b, 0)),
                  pl.BlockSpec((BLK, LANES), lambda b: (b, 0)),
                  pl.BlockSpec((1,) + s0t.shape[1:], smap4),
                  pl.BlockSpec((1, LANES), lambda b: (0, 0)),
                  pl.BlockSpec((1, LANES), lambda b: (0, 0)),
                  pl.BlockSpec((1, LANES), lambda b: (0, 0)),
                  pl.BlockSpec((1, inner), lambda b: (0, 0))],
        out_specs=[pl.BlockSpec((BLK, inner), lambda b: (b, 0)),
                   pl.BlockSpec((1,) + s0t.shape[1:], smap4)],
        out_shape=[jax.ShapeDtypeStruct((m, inner), BF16),
                   jax.ShapeDtypeStruct(s0t.shape, F32)],
        scratch_shapes=[pltpu.VMEM((BLK, inner), F32)],
        compiler_params=_cp(),
        name='ssd_scan',
    )(xc, z, dt, s0t, pad_h(dt_bias), pad_h(a_log), pad_h(d_skip), norm_w.reshape(1, inner))


def _conv_taps(a, prev8, cw_ref, cb, cs, width):
    groups = [prev8] + [a[g * SUBLANES:(g + 1) * SUBLANES] for g in range(a.shape[0] // SUBLANES)]
    first_row = lax.broadcasted_iota(jnp.int32, (SUBLANES, 1), 0) == 0
    w0 = cw_ref[0:1, cs]
    z = [x * w0 for x in groups]
    for j in range(1, width):
        wj = cw_ref[j:j + 1, cs]
        rot = [pltpu.roll(t, 1, 0) for t in z]
        z = [groups[0] * wj + rot[0]] + [groups[g] * wj + jnp.where(first_row, rot[g - 1], rot[g])
                                         for g in range(1, len(groups))]
    return jnp.concatenate(z[1:], axis=0) + cb


def _mm_conv_body(*refs, width, gated, nbp, lreal, dec_seq, tm, chunk):
    if gated:
        x_ref, w_ref, wu_ref, tail_ref, cw_ref, cb_ref, o_ref, st_ref, carry_ref = refs
    else:
        x_ref, w_ref, tail_ref, cw_ref, cb_ref, o_ref, st_ref, carry_ref = refs
    i = pl.program_id(0)
    nsub = tm // BLK
    n = w_ref.shape[1]
    nv_last = lreal - ((lreal - 1) // BLK) * BLK
    x = x_ref[...].astype(BF16)
    row = lax.broadcasted_iota(jnp.int32, (BLK, 1), 0)

    @pl.when(i == 0)
    def _():
        st_ref[...] = jnp.zeros_like(st_ref)
        carry_ref[...] = jnp.zeros_like(carry_ref)

    for c0 in range(0, n, chunk):
        cs = slice(c0, c0 + chunk)
        acc = jnp.dot(x, w_ref[:, cs], preferred_element_type=F32)
        if gated:
            up = jnp.dot(x, wu_ref[:, cs], preferred_element_type=F32)
        for j in range(nsub):
            blk = i * nsub + j
            is_start, nvalid = _block_info(blk, nbp, lreal, dec_seq)
            sid = jnp.maximum(blk - nbp + 1, 0)
            rows = slice(j * BLK, (j + 1) * BLK)
            a = acc[rows]
            before = carry_ref[:, cs] if j == 0 else acc[j * BLK - SUBLANES:j * BLK]
            prev8 = jnp.where(is_start, tail_ref[sid, :, cs], before)
            conv = _conv_taps(a, prev8, cw_ref, cb_ref[:, cs], cs, width)
            act = conv * _sigmoid(conv)
            if gated:
                o_ref[rows, cs] = (act * up[rows]).astype(o_ref.dtype)
            else:
                o_ref[rows, cs] = jnp.where(row < nvalid, act, 0.0).astype(o_ref.dtype)
            both = jnp.concatenate([prev8, a], axis=0)
            cand = jnp.where(blk >= nbp, both[dec_seq:dec_seq + SUBLANES],
                             jnp.where(nvalid == BLK, both[BLK:BLK + SUBLANES], both[nv_last:nv_last + SUBLANES]))
            st_ref[sid, :, cs] = jnp.where(nvalid > 0, cand, st_ref[sid, :, cs])
        carry_ref[:, cs] = acc[tm - SUBLANES:tm]


def _mm_conv(x, w, wu, tails, conv_w, conv_b, out_dtype, lay, name):
    m, k = x.shape
    n = w.shape[1]
    width = conv_w.shape[0]
    nstream = tails.shape[0]
    gated = wu is not None
    body = functools.partial(_mm_conv_body, width=width, gated=gated, nbp=lay['nbp'], lreal=lay['lreal'],
                             dec_seq=lay['dec_seq'], tm=TM, chunk=2 * LANES)
    w_spec = pl.BlockSpec((k, n), lambda i: (0, 0), pipeline_mode=pl.Buffered(1))
    full = lambda shape: pl.BlockSpec(shape, lambda i: (0,) * len(shape))
    return pl.pallas_call(
        body,
        grid=(m // TM,),
        in_specs=[pl.BlockSpec((TM, k), lambda i: (i, 0)), w_spec] + ([w_spec] if gated else [])
        + [full((nstream, SUBLANES, n)), full((width, n)), full((1, n))],
        out_specs=[pl.BlockSpec((TM, n), lambda i: (i, 0)), full((nstream, SUBLANES, n))],
        out_shape=[jax.ShapeDtypeStruct((m, n), out_dtype), jax.ShapeDtypeStruct((nstream, SUBLANES, n), F32)],
        scratch_shapes=[pltpu.VMEM((SUBLANES, n), F32)],
        compiler_params=_cp(),
        name=name,
    )(*([x, w] + ([wu] if gated else []) + [tails, conv_w, conv_b.reshape(1, n)]))


def _rope_body(qk_ref, v_ref, cos_ref, sin_ref, q16_ref, k32_ref, k16_ref, v16_ref, vt_ref, *, d_model, scale):
    cos = cos_ref[...]
    sin = sin_ref[...]
    lane = lax.broadcasted_iota(jnp.int32, cos.shape, 1)
    first_half = (lane % DA_HD) < (DA_HD // 2)
    for c in range(2 * d_model // LANES):
        x = qk_ref[:, c * LANES:(c + 1) * LANES]
        partner = jnp.where(first_half, -pltpu.roll(x, LANES - DA_HD // 2, 1), pltpu.roll(x, DA_HD // 2, 1))
        out = x * cos + partner * sin
        if c < d_model // LANES:
            q16_ref[:, c * LANES:(c + 1) * LANES] = (out * scale).astype(BF16)
        else:
            c2 = c - d_model // LANES
            k32_ref[:, c2 * LANES:(c2 + 1) * LANES] = out
            k16_ref[:, c2 * LANES:(c2 + 1) * LANES] = out.astype(BF16)
    v = v_ref[...]
    v16_ref[...] = v.astype(BF16)
    for h in range(d_model // LANES):
        vt_ref[h, 0] = v[:, h * LANES:(h + 1) * LANES].T.astype(BF16)


def _rope(qkv, cos, sin, d_model):
    m = qkv.shape[0]
    t = ATT_TK
    nh = d_model // LANES
    return pl.pallas_call(
        functools.partial(_rope_body, d_model=d_model, scale=DA_HD ** -0.5 * math.log2(math.e)),
        grid=(m // t,),
        in_specs=[pl.BlockSpec((t, 2 * d_model), lambda i: (i, 0)),
                  pl.BlockSpec((t, d_model), lambda i: (i, 2)),
                  pl.BlockSpec((t, LANES), lambda i: (i, 0)),
                  pl.BlockSpec((t, LANES), lambda i: (i, 0))],
        out_specs=[pl.BlockSpec((t, d_model), lambda i: (i, 0))] * 4
        + [pl.BlockSpec((nh, 1, LANES, t), lambda i: (0, i, 0, 0))],
        out_shape=[jax.ShapeDtypeStruct((m, d_model), BF16),
                   jax.ShapeDtypeStruct((m, d_model), F32),
                   jax.ShapeDtypeStruct((m, d_model), BF16),
                   jax.ShapeDtypeStruct((m, d_model), BF16),
                   jax.ShapeDtypeStruct((nh, m // t, LANES, t), BF16)],
        compiler_params=_cp(),
        name='rope',
    )(qkv, qkv, cos, sin)


def _diff_lambda(lam_ref, lam_init):
    lp = lam_ref[...]
    return (jnp.exp(jnp.sum(lp[0:1] * lp[1:2], axis=-1, keepdims=True))
            - jnp.exp(jnp.sum(lp[2:3] * lp[3:4], axis=-1, keepdims=True)) + lam_init)


def _attn_prompt_body(*refs, tq, lp, **kw):
    o_ref = refs[5]

    @pl.when(pl.program_id(1) < lp // tq)
    def _():
        _attn_prompt_tile(*refs, tq=tq, lp=lp, **kw)

    @pl.when(pl.program_id(1) >= lp // tq)
    def _():
        o_ref[...] = jnp.zeros_like(o_ref)


def _attn_prompt_tile(q_ref, k_ref, vt_ref, lam_ref, g_ref, o_ref, acc_ref, sa_ref, sb_ref, *, tq, tk, lp, lreal, off,
                      lam_init):
    qi = pl.program_id(1)
    nsub = tq // ATT_TQ_SUB
    chains = [(a, c) for a in range(nsub) for c in range(2)]
    qts, qchs = {}, {}
    for a in range(nsub):
        qt = q_ref[a * ATT_TQ_SUB:(a + 1) * ATT_TQ_SUB, :].astype(F32).T
        feat = lax.broadcasted_iota(jnp.int32, qt.shape, 0)
        qts[a, 0] = jnp.where(feat < DA_HD, qt, 0.0).astype(BF16)
        qts[a, 1] = jnp.where(feat >= DA_HD, qt, 0.0).astype(BF16)
        qpos = qi * tq + a * ATT_TQ_SUB + lax.broadcasted_iota(jnp.int32, (1, ATT_TQ_SUB), 1)
        qchs[a] = (qpos + off) >> 6
    acc_ref[...] = jnp.zeros_like(acc_ref)
    full_last = jnp.minimum(((qi * tq + off) >> 6) * CHUNK - off + CHUNK - 1, lreal - 1)
    any_last = jnp.minimum(((qi * tq + tq - 1 + off) >> 6) * CHUNK - off + CHUNK - 1, lreal - 1)
    nk = any_last // tk + 1
    nfull = jnp.minimum((full_last + 1) // tk, nk - 1)

    def scores(ki, s_ref, width=tk):
        kt = k_ref[pl.ds(pl.multiple_of(ki * tk, tk), width), :]
        for n, ch in enumerate(chains):
            s_ref[n, 0:width, :] = jnp.dot(kt, qts[ch], preferred_element_type=F32)

    def consume(ki, s_ref, carry, masked, width=tk):
        vt = vt_ref[0, ki, :, 0:width]
        s = [s_ref[n, 0:width, :] for n in range(len(chains))]
        if masked:
            kpos = ki * tk + lax.broadcasted_iota(jnp.int32, (width, 1), 0)
            for n, (a, _) in enumerate(chains):
                vis = jnp.logical_and(((kpos + off) >> 6) <= qchs[a], kpos < lreal)
                s[n] = jnp.where(vis, s[n], NEG_BIG)
        m_new = [jnp.maximum(carry[2 * n], jnp.max(s[n], axis=0, keepdims=True)) for n in range(len(chains))]
        p = [jnp.exp2(s[n] - m_new[n]) for n in range(len(chains))]
        new = []
        for n in range(len(chains)):
            alpha = jnp.exp2(carry[2 * n] - m_new[n])
            l_new = alpha * carry[2 * n + 1] + jnp.sum(p[n], axis=0, keepdims=True)
            acc_ref[n] = alpha * acc_ref[n] + jnp.dot(vt, p[n].astype(BF16), preferred_element_type=F32)
            new += [m_new[n], l_new]
        return tuple(new)

    npair = nfull // 2
    last_tile = nk - 1

    def pair(j, carry):
        scores(2 * j + 1, sb_ref)
        carry = consume(2 * j, sa_ref, carry, masked=False)
        scores(jnp.minimum(2 * j + 2, last_tile), sa_ref)
        return consume(2 * j + 1, sb_ref, carry, masked=False)

    def single(ki, carry, width=tk):
        scores(ki, sb_ref, width)
        return consume(ki, sb_ref, carry, True, width)

    carry = (jnp.full((1, ATT_TQ_SUB), NEG_BIG, F32), jnp.zeros((1, ATT_TQ_SUB), F32)) * len(chains)
    scores(0, sa_ref)
    carry = lax.fori_loop(0, npair, pair, carry)
    carry = lax.fori_loop(2 * npair, last_tile, single, carry)
    narrow = any_last - last_tile * tk < ATT_TK_TAIL
    carry = lax.cond(narrow, functools.partial(single, last_tile, width=ATT_TK_TAIL),
                     functools.partial(single, last_tile), carry)
    lam = _diff_lambda(lam_ref, lam_init)
    for a in range(nsub):
        l0, l1 = carry[4 * a + 1], carry[4 * a + 3]
        o = acc_ref[2 * a] / l0 - lam * (acc_ref[2 * a + 1] / l1)
        ms = jnp.mean(o * o, axis=0, keepdims=True)
        o = o * lax.rsqrt(ms + SUBLN_EPS) * g_ref[...] * (1.0 - lam_init)
        o_ref[a * ATT_TQ_SUB:(a + 1) * ATT_TQ_SUB, :] = o.T.astype(o_ref.dtype)


def _attn_sample_body(q_ref, kn_ref, vn_ref, kc_ref, vc_ref, lam_ref, g_ref, o_alias_ref, o_ref, acc_ref, *,
                      tk, past, dec_seq, lam_init):
    del o_alias_ref
    nq = -(-dec_seq // 16) * 16
    q = q_ref[0:nq, :]
    lane = lax.broadcasted_iota(jnp.int32, q.shape, 1)
    zero = jnp.zeros_like(q)
    q2 = jnp.concatenate([jnp.where(lane < DA_HD, q, zero), jnp.where(lane >= DA_HD, q, zero)], axis=0)
    acc_ref[...] = jnp.zeros_like(acc_ref)
    qrow = lax.broadcasted_iota(jnp.int32, (2 * nq, 1), 0)
    qch = (past + jnp.where(qrow >= nq, qrow - nq, qrow)) >> 6

    def tile(kt, vt, vis, carry):
        m_old, l_old = carry
        s = lax.dot_general(q2, kt, _NT, preferred_element_type=F32)
        s = jnp.where(vis, s, NEG_BIG)
        m_new = jnp.maximum(m_old, jnp.max(s, axis=-1, keepdims=True))
        p = jnp.exp2(s - m_new)
        alpha = jnp.exp2(m_old - m_new)
        l_new = alpha * l_old + jnp.sum(p, axis=-1, keepdims=True)
        acc_ref[...] = alpha * acc_ref[...] + jnp.dot(p.astype(BF16), vt, preferred_element_type=F32)
        return m_new, l_new

    def body(ki, c):
        ks = pl.multiple_of(ki * tk, tk)
        kt = kc_ref[0, pl.ds(ks, tk), :].astype(BF16)
        vt = vc_ref[0, pl.ds(ks, tk), :].astype(BF16)
        kpos = ks + lax.broadcasted_iota(jnp.int32, (1, tk), 1)
        return tile(kt, vt, (kpos >> 6) <= qch, c)

    carry = (jnp.full((2 * nq, 1), NEG_BIG, F32), jnp.zeros((2 * nq, 1), F32))
    carry = lax.fori_loop(0, past // tk, body, carry)
    kidx = lax.broadcasted_iota(jnp.int32, (1, nq), 1)
    vis = jnp.logical_and(((past + kidx) >> 6) <= qch, kidx < dec_seq)
    _, l = tile(kn_ref[0:nq, :], vn_ref[0:nq, :], vis, carry)
    lam = _diff_lambda(lam_ref, lam_init)
    on = acc_ref[...] / l
    o = on[:nq] - lam * on[nq:]
    ms = jnp.mean(o * o, axis=-1, keepdims=True)
    o_ref[...] = jnp.zeros_like(o_ref)
    o_ref[0:nq, :] = (o * lax.rsqrt(ms + SUBLN_EPS) * g_ref[...] * (1.0 - lam_init)).astype(o_ref.dtype)


def _attention(q16, k16, v16, vt16, k_cache, v_cache, lam_p, subln_g, lam_init, lay):
    m, d = q16.shape
    nh = d // LANES
    lp = lay['lp']
    ns = lay['ns']
    past = k_cache.shape[1]
    g = subln_g.astype(F32)
    lam_p = lam_p.astype(F32)
    nkt = vt16.shape[1]
    o_prompt = pl.pallas_call(
        functools.partial(_attn_prompt_body, tq=ATT_TQ, tk=ATT_TK, lp=lp, lreal=lay['lreal'], off=CHUNK - N_META,
                          lam_init=lam_init),
        grid=(nh, m // ATT_TQ),
        in_specs=[pl.BlockSpec((ATT_TQ, LANES), lambda h, i: (i, h)),
                  pl.BlockSpec((lp, LANES), lambda h, i: (0, h)),
                  pl.BlockSpec((1, nkt, LANES, ATT_TK), lambda h, i: (h, 0, 0, 0)),
                  pl.BlockSpec((4, DA_HD), lambda h, i: (0, 0)),
                  pl.BlockSpec((LANES, 1), lambda h, i: (0, 0))],
        out_specs=pl.BlockSpec((ATT_TQ, LANES), lambda h, i: (i, h)),
        out_shape=jax.ShapeDtypeStruct((m, d), BF16),
        scratch_shapes=[pltpu.VMEM((2 * ATT_TQ // ATT_TQ_SUB, LANES, ATT_TQ_SUB), F32),
                        pltpu.VMEM((2 * ATT_TQ // ATT_TQ_SUB, ATT_TK, ATT_TQ_SUB), F32),
                        pltpu.VMEM((2 * ATT_TQ // ATT_TQ_SUB, ATT_TK, ATT_TQ_SUB), F32)],
        compiler_params=_cp(),
        name='attn_prompt',
    )(q16, k16, vt16, lam_p, g.reshape(LANES, 1))
    nbp = lay['nbp']
    tk = min(ATT_TK_SAMPLE, past)
    nq = -(-lay['dec_seq'] // 16) * 16
    blk_spec = pl.BlockSpec((BLK, LANES), lambda s, h: (nbp + s, h))
    return pl.pallas_call(
        functools.partial(_attn_sample_body, tk=tk, past=past, dec_seq=lay['dec_seq'], lam_init=lam_init),
        grid=(ns, nh),
        in_specs=[blk_spec, blk_spec, blk_spec,
                  pl.BlockSpec((1, past, LANES), lambda s, h: (s, 0, h)),
                  pl.BlockSpec((1, past, LANES), lambda s, h: (s, 0, h)),
                  pl.BlockSpec((4, DA_HD), lambda s, h: (0, 0)),
                  pl.BlockSpec((1, LANES), lambda s, h: (0, 0)),
                  pl.BlockSpec(memory_space=pl.ANY)],
        out_specs=blk_spec,
        out_shape=jax.ShapeDtypeStruct((m, d), BF16),
        scratch_shapes=[pltpu.VMEM((2 * nq, LANES), F32)],
        input_output_aliases={7: 0},
        compiler_params=_cp(),
        name='attn_sample',
    )(q16, k16, v16, k_cache, v_cache, lam_p, g.reshape(1, LANES), o_prompt)


def _rk_in_body(x_ref, tail_ref, mu_ref, wr_ref, wk_ref, wv_ref, w1_ref, w2_ref, a1_ref, a2_ref, g1_ref, g2_ref,
                r_ref, k_ref, v_ref, lw_ref, la_ref, g_ref, carry_ref, *, nbp, lreal, dec_seq, tm):
    i = pl.program_id(0)

    @pl.when(i == 0)
    def _():
        carry_ref[...] = jnp.zeros_like(carry_ref)

    x = x_ref[...]
    shifted = []
    for j in range(tm // BLK):
        blk = i * (tm // BLK) + j
        is_start, _ = _block_info(blk, nbp, lreal, dec_seq)
        sid = jnp.maximum(blk - nbp + 1, 0)
        before = carry_ref[...] if j == 0 else x[j * BLK - SUBLANES:j * BLK]
        prev8 = jnp.where(is_start, tail_ref[sid], before)
        ext = jnp.concatenate([prev8, x[j * BLK:(j + 1) * BLK]], axis=0)
        shifted.append(ext[SUBLANES - 1:SUBLANES - 1 + BLK])
    carry_ref[...] = x[tm - SUBLANES:tm]
    xx = jnp.concatenate(shifted, axis=0) - x

    def mix(n):
        return (x + xx * mu_ref[n:n + 1, :]).astype(BF16)

    def mm(a, w_ref):
        return jnp.dot(a, w_ref[...], preferred_element_type=F32)

    r_ref[...] = mm(mix(0), wr_ref)
    lw_ref[...] = mm(jnp.tanh(mm(mix(1), w1_ref)).astype(BF16), w2_ref)
    k_ref[...] = mm(mix(2), wk_ref)
    v_ref[...] = mm(mix(3), wv_ref)
    la_ref[...] = mm(mm(mix(4), a1_ref).astype(BF16), a2_ref)
    g_ref[...] = mm(_sigmoid(mm(mix(5), g1_ref)).astype(BF16), g2_ref)


def _rk_in(x, tails, mu, w_r, w_k, w_v, w1, w2, a1, a2, g1, g2, lay):
    m, d = x.shape
    nstream = tails.shape[0]

    def lora_pair(u, w):
        r = u.shape[1]
        rp = -(-r // LANES) * LANES
        return jnp.pad(u, ((0, 0), (0, rp - r))).astype(BF16), jnp.pad(w, ((0, rp - r), (0, 0))).astype(BF16)

    ws = [w_r.astype(BF16), w_k.astype(BF16), w_v.astype(BF16), *lora_pair(w1, w2), *lora_pair(a1, a2),
          *lora_pair(g1, g2)]
    full = lambda a: pl.BlockSpec(a.shape, lambda i: (0,) * a.ndim, pipeline_mode=pl.Buffered(1))
    row = pl.BlockSpec((TM, d), lambda i: (i, 0))
    body = functools.partial(_rk_in_body, nbp=lay['nbp'], lreal=lay['lreal'], dec_seq=lay['dec_seq'], tm=TM)
    return pl.pallas_call(
        body,
        grid=(m // TM,),
        in_specs=[row, full(tails), full(mu)] + [full(w) for w in ws],
        out_specs=[row] * 6,
        out_shape=[jax.ShapeDtypeStruct((m, d), F32)] * 6,
        scratch_shapes=[pltpu.VMEM((SUBLANES, d), F32)],
        compiler_params=_cp(),
        name='rk_in',
    )(x, tails, mu, *ws)


def _half_sum(x, low):
    s_lo = jnp.sum(jnp.where(low, x, 0.0), axis=-1, keepdims=True)
    s_hi = jnp.sum(jnp.where(low, 0.0, x), axis=-1, keepdims=True)
    return jnp.where(low, s_lo, s_hi)


def _wkv_body(r_ref, k_ref, v_ref, lw_ref, la_ref, g_ref, s0_ref,
              w0_ref, a0_ref, kk_ref, ka_ref, lng_ref, lnb_ref, rk_ref,
              o_ref, st_ref, stage_ref, yb_ref, pw_ref, x_ref, ak_ref, dd_ref, es_ref, bke_ref, gc_ref,
              *, nbp, lreal, dec_seq):
    c = BLK
    npair = st_ref.shape[1]
    half = c // 2
    b = pl.program_id(0)
    is_start, nvalid = _block_info(b, nbp, lreal, dec_seq)

    @pl.when(is_start)
    def _():
        st_ref[0] = s0_ref[0]

    for i, ref in enumerate((r_ref, k_ref, v_ref, lw_ref, la_ref, g_ref)):
        for p in range(npair):
            stage_ref[i, p] = ref[:, p * LANES:(p + 1) * LANES]

    row = lax.broadcasted_iota(jnp.int32, (c, 1), 0)
    valid = row < nvalid
    lane = lax.broadcasted_iota(jnp.int32, (c, LANES), 1)
    low = lane < RK_HD
    ri = lax.broadcasted_iota(jnp.int32, (c, c), 0)
    ci = lax.broadcasted_iota(jnp.int32, (c, c), 1)
    lower_incl = ri >= ci
    lower_strict = ri > ci
    tri = lower_incl.astype(F32)
    eye = (ri == ci).astype(F32)
    same_head = (ri < RK_HD) == (ci < RK_HD)

    prep_group = 4

    def each(f, *cols):
        return [f(*args) for args in zip(*cols)]

    def prep(g, carry):
        ps = [g * prep_group + i for i in range(prep_group)]
        r = [stage_ref[0, p] for p in ps]
        k = [stage_ref[1, p] for p in ps]
        v = [jnp.where(valid, stage_ref[2, p], 0.0) for p in ps]
        w_log = [-_softplus(-(w0_ref[p] + stage_ref[3, p])) - 0.5 for p in ps]
        logw = each(lambda t: jnp.where(valid, -jnp.exp(t), 0.0), w_log)
        ag = [_sigmoid(a0_ref[p] + stage_ref[4, p]) for p in ps]
        kk = [ki * kk_ref[p] for ki, p in zip(k, ps)]
        kk = each(lambda t: t / jnp.maximum(jnp.sqrt(_half_sum(t * t, low)), 1e-12), kk)
        kxm = [jnp.where(valid, ki * (1.0 + (gi - 1.0) * ka_ref[p]), 0.0) for ki, gi, p in zip(k, ag, ps)]
        a = each(lambda t: jnp.where(valid, -t, 0.0), kk)
        bv = each(lambda t, gi: jnp.where(valid, t * gi, 0.0), kk, ag)

        cum = each(lambda t: jnp.dot(tri, t, preferred_element_type=F32, precision=lax.Precision.HIGHEST), logw)
        e_cum = each(jnp.exp, cum)
        e_prev = each(lambda ci_, li: jnp.exp(ci_ - li), cum, logw)
        e_neg = each(lambda ci_: jnp.exp(-ci_), cum)
        mid_dn = each(lambda ci_: jnp.exp(-ci_[half - 1:half, :]), cum)
        mid_up = each(lambda ci_: jnp.exp(ci_[half - 1:half, :]), cum)
        end_up = each(lambda ci_: jnp.exp(ci_[c - 1:c, :]), cum)
        a_abs = each(lambda x_, e: x_ * e, a, e_prev)
        r_abs = each(lambda x_, e: x_ * e, r, e_cum)
        bn = each(lambda x_, e: x_ * e, bv, e_neg)
        kn = each(lambda x_, e: x_ * e, kxm, e_neg)

        for i, p in enumerate(ps):
            stage_ref[1, p] = kxm[i]
            stage_ref[2, p] = v[i]
            bke_ref[p] = jnp.concatenate([bn[i] * end_up[i], kn[i] * end_up[i]], axis=0).astype(BF16)
            gc_ref[p] = jnp.broadcast_to(end_up[i], (SUBLANES, LANES))
        for i, p in enumerate(ps):
            s = st_ref[0, p]
            es_ref[p] = _bdot_nt(jnp.concatenate([a_abs[i], r_abs[i]], axis=0), s)
        for i, p in enumerate(ps):
            a_rel = a_abs[i] * mid_dn[i]
            r_rel = r_abs[i] * mid_dn[i]
            bk_rel = jnp.concatenate([bn[i] * mid_up[i], kn[i] * mid_up[i]], axis=0).astype(BF16)
            for j, sel in enumerate((low, jnp.logical_not(low))):
                ar = jnp.concatenate([jnp.where(sel, a_rel, 0.0), jnp.where(sel, r_rel, 0.0)], axis=0)
                mm = lax.dot_general(ar.astype(BF16), bk_rel, _NT, preferred_element_type=F32)
                n = jnp.where(lower_strict, mm[:c, :c], 0.0)
                h = 2 * p + j
                pw_ref[h] = n.astype(BF16)
                x_ref[h] = eye + n
                ak_ref[h] = jnp.where(lower_strict, mm[:c, c:], 0.0).astype(BF16)
                dd_ref[h] = jnp.concatenate([jnp.where(lower_incl, mm[c:, :c], 0.0),
                                             jnp.where(lower_incl, mm[c:, c:], 0.0)], axis=1).astype(BF16)
        return carry

    lax.fori_loop(0, npair // prep_group, prep, 0)

    nlev = int(math.log2(c))
    inv_group = 8

    def invert(g, carry):
        hs = [g * inv_group + j for j in range(inv_group)]
        for h in hs:
            pw = pw_ref[h]
            pw_ref[h] = jnp.dot(pw, pw, preferred_element_type=F32).astype(BF16)
        for _ in range(1, nlev - 1):
            for h in hs:
                pw = pw_ref[h]
                both = jnp.dot(jnp.concatenate([pw, x_ref[h].astype(BF16)], axis=0), pw,
                               preferred_element_type=F32)
                pw_ref[h] = both[:c].astype(BF16)
                x_ref[h] = x_ref[h] + both[c:]
        for h in hs:
            x = x_ref[h]
            x_ref[h] = x + jnp.dot(x.astype(BF16), pw_ref[h], preferred_element_type=F32)
        return carry

    lax.fori_loop(0, 2 * npair // inv_group, invert, 0)

    out_group = 4

    def emit(g, carry):
        ps = [g * out_group + i for i in range(out_group)]
        hs = [(i, j) for i in range(out_group) for j in range(2)]
        rhs, us, ys = {}, {}, {}
        for i, j in hs:
            p = ps[i]
            rhs[i, j] = es_ref[p, 0:c, :] + jnp.dot(ak_ref[2 * p + j], stage_ref[2, p].astype(BF16),
                                                    preferred_element_type=F32)
        for i, j in hs:
            us[i, j] = _bdot(x_ref[2 * ps[i] + j], rhs[i, j])
        for i, j in hs:
            p = ps[i]
            uv = jnp.concatenate([us[i, j], stage_ref[2, p]], axis=0).astype(BF16)
            ys[i, j] = es_ref[p, c:2 * c, :] + jnp.dot(dd_ref[2 * p + j], uv, preferred_element_type=F32)
        for i, p in enumerate(ps):
            v = stage_ref[2, p]
            u = jnp.where(low, us[i, 0], us[i, 1])
            y = jnp.where(low, ys[i, 0], ys[i, 1])
            uv_t = jnp.concatenate([u, v], axis=0).T.astype(BF16)
            upd = jnp.dot(uv_t, bke_ref[p], preferred_element_type=F32)
            st_ref[0, p] = st_ref[0, p] * gc_ref[p, 0:1, :] + jnp.where(same_head, upd, 0.0)

            mean = _half_sum(y, low) * (1.0 / RK_HD)
            dy = y - mean
            var = _half_sum(dy * dy, low) * (1.0 / RK_HD)
            yn = dy * lax.rsqrt(var + RK_GN_EPS) * lng_ref[p] + lnb_ref[p]
            bonus = _half_sum(stage_ref[0, p] * stage_ref[1, p] * rk_ref[p], low) * v
            yb_ref[p] = ((yn + bonus) * stage_ref[5, p]).astype(yb_ref.dtype)
        return carry

    lax.fori_loop(0, npair // out_group, emit, 0)
    for p in range(npair):
        o_ref[:, p * LANES:(p + 1) * LANES] = yb_ref[p]


def _wkv(r, k, v, lw, la, gate, s0, w0, a0, k_k, k_a, lnx_g, lnx_b, r_k, lay):
    m, d = r.shape
    npair = d // LANES

    def prm(t):
        return t.astype(F32).reshape(npair, 1, LANES)

    body = functools.partial(_wkv_body, nbp=lay['nbp'], lreal=lay['lreal'], dec_seq=lay['dec_seq'])
    row_spec = pl.BlockSpec((BLK, d), lambda b: (b, 0))
    prm_spec = pl.BlockSpec((npair, 1, LANES), lambda b: (0, 0, 0))
    smap = _stream_map(lay['nbp'], 4)
    return pl.pallas_call(
        body,
        grid=(m // BLK,),
        in_specs=[row_spec] * 6 + [pl.BlockSpec((1,) + s0.shape[1:], smap)] + [prm_spec] * 7,
        out_specs=[row_spec, pl.BlockSpec((1,) + s0.shape[1:], smap)],
        out_shape=[jax.ShapeDtypeStruct((m, d), BF16), jax.ShapeDtypeStruct(s0.shape, F32)],
        scratch_shapes=[pltpu.VMEM((6, npair, BLK, LANES), F32),
                        pltpu.VMEM((npair, BLK, LANES), BF16),
                        pltpu.VMEM((2 * npair, BLK, BLK), BF16),
                        pltpu.VMEM((2 * npair, BLK, BLK), F32),
                        pltpu.VMEM((2 * npair, BLK, BLK), BF16),
                        pltpu.VMEM((2 * npair, BLK, 2 * BLK), BF16),
                        pltpu.VMEM((npair, 2 * BLK, LANES), F32),
                        pltpu.VMEM((npair, 2 * BLK, LANES), BF16),
                        pltpu.VMEM((npair, SUBLANES, LANES), F32)],
        compiler_params=_cp(),
        name='wkv7',
    )(r, k, v, lw, la, gate, s0, prm(w0), prm(a0), prm(k_k), prm(k_a), prm(lnx_g), prm(lnx_b), prm(r_k))


def _tails(prev, nstream):
    ns, w, c = prev.shape
    t = jnp.zeros((nstream, SUBLANES, c), F32)
    return t.at[1:, SUBLANES - w:, :].set(prev.astype(F32))


def _last_rows(flat, w, lay):
    lreal, lp, ns, dec = lay['lreal'], lay['lp'], lay['ns'], lay['dec_seq']
    p = flat[lreal - w:lreal][None]
    s = flat[lp:].reshape(ns, BLK, -1)[:, dec - w:dec]
    return p, s


def kernel(x_prompt, x_sample, cache_attn_k, cache_attn_v, state_ssm, state_ssd_conv, state_wkv, state_rwkv_shift, state_ffn_conv, meta_tokens, ssd_w_in, ssd_conv_w, ssd_conv_b, ssd_dt_bias, ssd_a_log, ssd_d, ssd_norm_w, ssd_w_out, da_w_qkv, da_lambda, da_subln_g, da_w_o, rk_mu, rk_w0, rk_w1, rk_w2, rk_a0, rk_a1, rk_a2, rk_g1, rk_g2, rk_k_k, rk_k_a, rk_r_k, rk_w_r, rk_w_k, rk_w_v, rk_w_o, rk_lnx_g, rk_lnx_b, ffn_w_up, ffn_w_gate, ffn_conv_w, ffn_conv_b, ffn_w_down, ln_g, ln_b):
    bp, seq, d = x_prompt.shape
    ns, dec_seq, _ = x_sample.shape
    depth = ln_g.shape[0]
    past = cache_attn_k.shape[2]
    assert bp == 1 and dec_seq >= SSD_CONV - 1 and dec_seq <= BLK and past % min(ATT_TK_SAMPLE, past) == 0
    lreal = N_META + seq
    row_align = max(TM, ATT_TQ, ATT_TK)
    lp = -(-lreal // row_align) * row_align
    m = lp + ns * BLK
    assert m % row_align == 0
    nstream = ns + 1
    lay = dict(lreal=lreal, lp=lp, nbp=lp // BLK, ns=ns, dec_seq=dec_seq)
    alpha = (2 * depth) ** 0.25

    x = jnp.concatenate([
        meta_tokens.astype(F32), x_prompt[0], jnp.zeros((lp - lreal, d), F32),
        jnp.pad(x_sample, ((0, 0), (0, BLK - dec_seq), (0, 0))).reshape(ns * BLK, d)], axis=0)

    pos = jnp.concatenate([jnp.arange(lp), jnp.tile(past + jnp.arange(BLK), ns)]).astype(F32)
    half = DA_HD // 2
    inv = ROPE_THETA ** (-jnp.arange(half, dtype=F32) / half)
    ang = pos[:, None] * inv[None, :]
    cos_t = jnp.tile(jnp.cos(ang), (1, LANES // half))
    sin_t = jnp.tile(jnp.sin(ang), (1, LANES // half))

    inner = ssd_norm_w.shape[1]
    nheads = ssd_dt_bias.shape[1]
    hpg = nheads // SSD_GROUPS
    new = {n: [] for n in ('k', 'v', 'ssm', 'ssd_conv', 'wkv', 'shift', 'ffn_conv')}

    for i in range(depth):
        j, kind = i // N_MIXERS, i % N_MIXERS
        if kind == 0:
            w_in = ssd_w_in[j]
            cdim = ssd_conv_w.shape[2]
            w_z = w_in[:, :inner].astype(BF16)
            w_xbc = w_in[:, inner:inner + cdim].astype(BF16)
            w_dt = jnp.pad(w_in[:, inner + cdim:], ((0, 0), (0, LANES - nheads))).astype(BF16)
            z, dt = _mm_multi(x, [w_z, w_dt], [F32, F32], 'ssd_in_z_dt')
            xc, conv_rows = _mm_conv(x, w_xbc, None, _tails(state_ssd_conv[j], nstream), ssd_conv_w[j],
                                     ssd_conv_b[j], F32, lay, 'ssd_in_xbc_conv')
            s0 = jnp.concatenate([jnp.zeros((1,) + state_ssm.shape[2:], F32), state_ssm[j].astype(F32)], axis=0)
            s0t = s0.reshape(nstream, SSD_GROUPS, hpg * SSD_HEADDIM, SSD_STATE).swapaxes(2, 3)
            y, st = _ssd_scan(xc, z, dt, s0t, ssd_dt_bias[j], ssd_a_log[j], ssd_d[j], ssd_norm_w[j], lay)
            w_o = ssd_w_out[j].astype(BF16)
            conv_rows = conv_rows[:, SUBLANES - (SSD_CONV - 1):]
            new['ssd_conv'].append((conv_rows[:1], conv_rows[1:]))
            ssm = st.swapaxes(2, 3).reshape(nstream, nheads, SSD_HEADDIM, SSD_STATE)
            new['ssm'].append((ssm[:1], ssm[1:]))
        elif kind == 1:
            lam_init = 0.8 - 0.6 * math.exp(-0.3 * i)
            (qkv,) = _mm_multi(x, [da_w_qkv[j].astype(BF16)], [F32], 'da_qkv')
            q16, k32, k16, v16, vt16 = _rope(qkv, cos_t, sin_t, d)
            nh = d // (2 * DA_HD)
            y = _attention(q16, k16, v16, vt16, cache_attn_k[j].reshape(ns, past, d),
                           cache_attn_v[j].reshape(ns, past, d), da_lambda[j], da_subln_g[j], lam_init, lay)
            w_o = da_w_o[j].astype(BF16)
            kp, ks = k32[:lreal][None], k32[lp:].reshape(ns, BLK, d)[:, :dec_seq]
            vp = qkv[:lreal, 2 * d:][None]
            vs = qkv[lp:, 2 * d:].reshape(ns, BLK, d)[:, :dec_seq]
            new['k'].append((kp.reshape(1, lreal, nh, 2, DA_HD), ks.reshape(ns, dec_seq, nh, 2, DA_HD)))
            new['v'].append((vp.reshape(1, lreal, nh, 2 * DA_HD), vs.reshape(ns, dec_seq, nh, 2 * DA_HD)))
        else:
            r, k, v, lw, la, gate = _rk_in(x, _tails(state_rwkv_shift[j], nstream), rk_mu[j], rk_w_r[j], rk_w_k[j],
                                           rk_w_v[j], rk_w1[j], rk_w2[j], rk_a1[j], rk_a2[j], rk_g1[j], rk_g2[j], lay)
            nrh = d // RK_HD
            npair = nrh // 2
            sw = jnp.concatenate([jnp.zeros((1,) + state_wkv.shape[2:], F32), state_wkv[j].astype(F32)], axis=0)
            sw = sw.reshape(nstream, npair, 2, RK_HD, RK_HD)
            zero = jnp.zeros_like(sw[:, :, 0])
            s0 = jnp.concatenate([jnp.concatenate([sw[:, :, 0], zero], axis=-1),
                                  jnp.concatenate([zero, sw[:, :, 1]], axis=-1)], axis=-2)
            y, st = _wkv(r, k, v, lw, la, gate, s0, rk_w0[j], rk_a0[j], rk_k_k[j], rk_k_a[j], rk_lnx_g[j],
                         rk_lnx_b[j], rk_r_k[j].reshape(-1), lay)
            w_o = rk_w_o[j].astype(BF16)
            wkv = jnp.stack([st[:, :, :RK_HD, :RK_HD], st[:, :, RK_HD:, RK_HD:]], axis=2).reshape(
                nstream, nrh, RK_HD, RK_HD)
            new['wkv'].append((wkv[:1], wkv[1:]))
            new['shift'].append(_last_rows(x, 1, lay))
        x = _mm_ln(y, w_o, x, ln_g[i, 0], ln_b[i, 0], alpha, 'mixer_out_ln')
        h, gate_rows = _mm_conv(x, ffn_w_gate[i].astype(BF16), ffn_w_up[i].astype(BF16),
                                _tails(state_ffn_conv[i], nstream), ffn_conv_w[i], ffn_conv_b[i], BF16, lay, 'ffn_in_conv')
        gate_rows = gate_rows[:, SUBLANES - (FFN_CONV - 1):]
        new['ffn_conv'].append((gate_rows[:1], gate_rows[1:]))
        x = _mm_ln(h, ffn_w_down[i].astype(BF16), x, ln_g[i, 1], ln_b[i, 1], alpha, 'ffn_out_ln')

    y_prompt = x[N_META:lreal][None]
    y_sample = x[lp:].reshape(ns, BLK, d)[:, :dec_seq]
    order = ('k', 'v', 'ssm', 'ssd_conv', 'wkv', 'shift', 'ffn_conv')
    prompt_states = tuple(jnp.stack([t[0] for t in new[n]]) for n in order)
    sample_states = tuple(jnp.stack([t[1] for t in new[n]]) for n in order)
    return (y_prompt, y_sample) + prompt_states + sample_states
```

```python
import functools
import math

import jax
import jax.numpy as jnp
from jax import lax
from jax.experimental import pallas as pl
from jax.experimental.pallas import tpu as pltpu

F32 = jnp.float32
BF16 = jnp.bfloat16

CHUNK = 64
N_META = 16
N_MIXERS = 3
SSD_HEADDIM = 64
SSD_GROUPS = 8
SSD_STATE = 128
SSD_CONV = 4
SSD_NORM_EPS = 1e-5
DA_HD = 64
ROPE_THETA = 10000.0
SUBLN_EPS = 1e-5
RK_HD = 64
RK_GN_EPS = 64e-5
FFN_CONV = 3
LN_EPS = 1e-5

LANES = 128
SUBLANES = 8
BLK = 128
TM = 512
ATT_TQ = 512
ATT_TQ_SUB = 256
ATT_TK = 512
ATT_TK_TAIL = 128
ATT_TK_SAMPLE = 4096
VMEM_LIMIT = 56 * 1024 * 1024
NEG_BIG = -1e30

_NT = (((1,), (1,)), ((), ()))


def _cp():
    return pltpu.CompilerParams(vmem_limit_bytes=VMEM_LIMIT)


def _sigmoid(x):
    return 0.5 + 0.5 * jnp.tanh(0.5 * x)


def _softplus(x):
    return jnp.maximum(x, 0.0) + jnp.log(1.0 + jnp.exp(-jnp.abs(x)))


def _bdot(a, b):
    return jnp.dot(a.astype(BF16), b.astype(BF16), preferred_element_type=F32)


def _bdot_nt(a, b):
    return lax.dot_general(a.astype(BF16), b.astype(BF16), _NT, preferred_element_type=F32)


def _mm_multi_body(*refs, n):
    x = refs[0][...].astype(BF16)
    for i in range(n):
        refs[1 + n + i][...] = jnp.dot(x, refs[1 + i][...], preferred_element_type=F32).astype(refs[1 + n + i].dtype)


def _mm_multi(x, ws, out_dtypes, name):
    m, k = x.shape
    n = len(ws)
    outs = pl.pallas_call(
        functools.partial(_mm_multi_body, n=n),
        grid=(m // TM,),
        in_specs=[pl.BlockSpec((TM, k), lambda i: (i, 0))]
        + [pl.BlockSpec(w.shape, lambda i: (0, 0), pipeline_mode=pl.Buffered(1)) for w in ws],
        out_specs=[pl.BlockSpec((TM, w.shape[1]), lambda i: (i, 0)) for w in ws],
        out_shape=[jax.ShapeDtypeStruct((m, w.shape[1]), dt) for w, dt in zip(ws, out_dtypes)],
        compiler_params=_cp(),
        name=name,
    )(x, *ws)
    return outs


def _mm_ln_body(x_ref, w_ref, r_ref, g_ref, b_ref, o_ref, *, alpha):
    acc = jnp.dot(x_ref[...].astype(BF16), w_ref[...], preferred_element_type=F32)
    h = alpha * r_ref[...] + acc
    mu = jnp.mean(h, axis=-1, keepdims=True)
    d = h - mu
    var = jnp.mean(d * d, axis=-1, keepdims=True)
    o_ref[...] = d * lax.rsqrt(var + LN_EPS) * g_ref[...] + b_ref[...]


def _mm_ln(x, w, resid, g, b, alpha, name):
    m, k = x.shape
    d = w.shape[1]
    return pl.pallas_call(
        functools.partial(_mm_ln_body, alpha=alpha),
        grid=(m // TM,),
        in_specs=[pl.BlockSpec((TM, k), lambda i: (i, 0)),
                  pl.BlockSpec((k, d), lambda i: (0, 0), pipeline_mode=pl.Buffered(1)),
                  pl.BlockSpec((TM, d), lambda i: (i, 0)),
                  pl.BlockSpec((1, d), lambda i: (0, 0)),
                  pl.BlockSpec((1, d), lambda i: (0, 0))],
        out_specs=pl.BlockSpec((TM, d), lambda i: (i, 0)),
        out_shape=jax.ShapeDtypeStruct((m, d), F32),
        compiler_params=_cp(),
        name=name,
    )(x, w, resid, g.reshape(1, d), b.reshape(1, d))


def _block_info(b, nbp, lreal, dec_seq):
    is_start = jnp.logical_or(b == 0, b >= nbp)
    nvalid = jnp.where(b < nbp, jnp.clip(lreal - b * BLK, 0, BLK), dec_seq)
    return is_start, nvalid


def _stream_map(nbp, ndim):
    def index_map(b):
        return (jnp.maximum(b - nbp + 1, 0),) + (0,) * (ndim - 1)
    return index_map


def _expand_heads(colmat, g, lane_group):
    out = colmat[:, 4 * g + 3:4 * g + 4]
    for r in (2, 1, 0):
        out = jnp.where(lane_group == r, colmat[:, 4 * g + r:4 * g + r + 1], out)
    return out


def _ssd_body(xc_ref, z_ref, dt_ref, s0_ref, dtb_ref, alog_ref, d_ref, nw_ref,
              y_ref, st_ref, yb_ref, *, nbp, lreal, dec_seq):
    q = BLK
    inner = z_ref.shape[1]
    gn = SSD_GROUPS * SSD_STATE
    b = pl.program_id(0)
    is_start, nvalid = _block_info(b, nbp, lreal, dec_seq)

    @pl.when(is_start)
    def _():
        st_ref[0] = s0_ref[0]

    row = lax.broadcasted_iota(jnp.int32, (q, 1), 0)
    valid = row < nvalid
    dt = _softplus(dt_ref[...] + dtb_ref[...])
    dt = jnp.where(valid, dt, 0.0)
    a = -jnp.exp(alog_ref[...])
    adt = dt * a
    ri = lax.broadcasted_iota(jnp.int32, (q, q), 0)
    ci = lax.broadcasted_iota(jnp.int32, (q, q), 1)
    causal = ri >= ci
    tri = causal.astype(F32)
    acs = jnp.dot(tri, adt, preferred_element_type=F32, precision=lax.Precision.HIGHEST)
    acs_t = acs.T
    last = acs[q - 1:q, :]
    lane_group = lax.broadcasted_iota(jnp.int32, (1, 4 * SSD_HEADDIM), 1) // SSD_HEADDIM

    ssq = jnp.zeros((q, 1), F32)
    for g in range(SSD_GROUPS):
        bb = xc_ref[:, inner + g * SSD_STATE:inner + (g + 1) * SSD_STATE]
        cc = xc_ref[:, inner + gn + g * SSD_STATE:inner + gn + (g + 1) * SSD_STATE]
        xs = xc_ref[:, g * 256:(g + 1) * 256]
        cbm = jnp.where(causal, _bdot_nt(cc, bb), 0.0)
        dt_x = _expand_heads(dt, g, lane_group)
        acs_x = _expand_heads(acs, g, lane_group)
        last_x = _expand_heads(last, g, lane_group)
        xdt = xs * dt_x
        st = st_ref[0, g]
        y = _bdot(cc, st) * jnp.exp(acs_x)
        for r in range(4):
            h = 4 * g + r
            seg = acs[:, h:h + 1] - acs_t[h:h + 1, :]
            att = cbm * jnp.exp(jnp.minimum(seg, 0.0))
            y = y + _bdot(att, jnp.where(lane_group == r, xdt, 0.0))
        xdtd = xdt * jnp.exp(last_x - acs_x)
        st_ref[0, g] = st * jnp.exp(last_x) + _bdot(bb.T, xdtd)
        y = y + _expand_heads(d_ref[...], g, lane_group) * xs
        zz = z_ref[:, g * 256:(g + 1) * 256]
        y = y * (zz * _sigmoid(zz))
        ssq = ssq + jnp.sum(y * y, axis=-1, keepdims=True)
        yb_ref[:, g * 256:(g + 1) * 256] = y
    scale = lax.rsqrt(ssq / inner + SSD_NORM_EPS)
    y_ref[...] = (yb_ref[...] * scale * nw_ref[...]).astype(y_ref.dtype)


def _ssd_scan(xc, z, dt, s0t, dt_bias, a_log, d_skip, norm_w, lay):
    m, cdim = xc.shape
    inner = z.shape[1]
    nh = dt_bias.shape[0]

    def pad_h(v):
        return jnp.pad(v.astype(F32), (0, LANES - nh)).reshape(1, LANES)

    body = functools.partial(_ssd_body, nbp=lay['nbp'], lreal=lay['lreal'], dec_seq=lay['dec_seq'])
    smap4 = _stream_map(lay['nbp'], 4)
    return pl.pallas_call(
        body,
        grid=(m // BLK,),
        in_specs=[pl.BlockSpec((BLK, cdim), lambda b: (b, 0)),
                  pl.BlockSpec((BLK, inner), lambda b: (b, 0)),
                  pl.BlockSpec((BLK, LANES), lambda b: (b, 0)),
                  pl.BlockSpec((1,) + s0t.shape[1:], smap4),
                  pl.BlockSpec((1, LANES), lambda b: (0, 0)),
                  pl.BlockSpec((1, LANES), lambda b: (0, 0)),
                  pl.BlockSpec((1, LANES), lambda b: (0, 0)),
                  pl.BlockSpec((1, inner), lambda b: (0, 0))],
        out_specs=[pl.BlockSpec((BLK, inner), lambda b: (b, 0)),
                   pl.BlockSpec((1,) + s0t.shape[1:], smap4)],
        out_shape=[jax.ShapeDtypeStruct((m, inner), BF16),
                   jax.ShapeDtypeStruct(s0t.shape, F32)],
        scratch_shapes=[pltpu.VMEM((BLK, inner), F32)],
        compiler_params=_cp(),
        name='ssd_scan',
    )(xc, z, dt, s0t, pad_h(dt_bias), pad_h(a_log), pad_h(d_skip), norm_w.reshape(1, inner))


def _conv_taps(a, prev8, cw_ref, cb, cs, width):
    groups = [prev8] + [a[g * SUBLANES:(g + 1) * SUBLANES] for g in range(a.shape[0] // SUBLANES)]
    first_row = lax.broadcasted_iota(jnp.int32, (SUBLANES, 1), 0) == 0
    w0 = cw_ref[0:1, cs]
    z = [x * w0 for x in groups]
    for j in range(1, width):
        wj = cw_ref[j:j + 1, cs]
        rot = [pltpu.roll(t, 1, 0) for t in z]
        z = [groups[0] * wj + rot[0]] + [groups[g] * wj + jnp.where(first_row, rot[g - 1], rot[g])
                                         for g in range(1, len(groups))]
    return jnp.concatenate(z[1:], axis=0) + cb


def _mm_conv_body(*refs, width, gated, nbp, lreal, dec_seq, tm, chunk):
    if gated:
        x_ref, w_ref, wu_ref, tail_ref, cw_ref, cb_ref, o_ref, st_ref, carry_ref = refs
    else:
        x_ref, w_ref, tail_ref, cw_ref, cb_ref, o_ref, st_ref, carry_ref = refs
    i = pl.program_id(0)
    nsub = tm // BLK
    n = w_ref.shape[1]
    nv_last = lreal - ((lreal - 1) // BLK) * BLK
    x = x_ref[...].astype(BF16)
    row = lax.broadcasted_iota(jnp.int32, (BLK, 1), 0)

    @pl.when(i == 0)
    def _():
        st_ref[...] = jnp.zeros_like(st_ref)
        carry_ref[...] = jnp.zeros_like(carry_ref)

    for c0 in range(0, n, chunk):
        cs = slice(c0, c0 + chunk)
        acc = jnp.dot(x, w_ref[:, cs], preferred_element_type=F32)
        if gated:
            up = jnp.dot(x, wu_ref[:, cs], preferred_element_type=F32)
        for j in range(nsub):
            blk = i * nsub + j
            is_start, nvalid = _block_info(blk, nbp, lreal, dec_seq)
            sid = jnp.maximum(blk - nbp + 1, 0)
            rows = slice(j * BLK, (j + 1) * BLK)
            a = acc[rows]
            before = carry_ref[:, cs] if j == 0 else acc[j * BLK - SUBLANES:j * BLK]
            prev8 = jnp.where(is_start, tail_ref[sid, :, cs], before)
            conv = _conv_taps(a, prev8, cw_ref, cb_ref[:, cs], cs, width)
            act = conv * _sigmoid(conv)
            if gated:
                o_ref[rows, cs] = (act * up[rows]).astype(o_ref.dtype)
            else:
                o_ref[rows, cs] = jnp.where(row < nvalid, act, 0.0).astype(o_ref.dtype)
            both = jnp.concatenate([prev8, a], axis=0)
            cand = jnp.where(blk >= nbp, both[dec_seq:dec_seq + SUBLANES],
                             jnp.where(nvalid == BLK, both[BLK:BLK + SUBLANES], both[nv_last:nv_last + SUBLANES]))
            st_ref[sid, :, cs] = jnp.where(nvalid > 0, cand, st_ref[sid, :, cs])
        carry_ref[:, cs] = acc[tm - SUBLANES:tm]


def _mm_conv(x, w, wu, tails, conv_w, conv_b, out_dtype, lay, name):
    m, k = x.shape
    n = w.shape[1]
    width = conv_w.shape[0]
    nstream = tails.shape[0]
    gated = wu is not None
    body = functools.partial(_mm_conv_body, width=width, gated=gated, nbp=lay['nbp'], lreal=lay['lreal'],
                             dec_seq=lay['dec_seq'], tm=TM, chunk=2 * LANES)
    w_spec = pl.BlockSpec((k, n), lambda i: (0, 0), pipeline_mode=pl.Buffered(1))
    full = lambda shape: pl.BlockSpec(shape, lambda i: (0,) * len(shape))
    return pl.pallas_call(
        body,
        grid=(m // TM,),
        in_specs=[pl.BlockSpec((TM, k), lambda i: (i, 0)), w_spec] + ([w_spec] if gated else [])
        + [full((nstream, SUBLANES, n)), full((width, n)), full((1, n))],
        out_specs=[pl.BlockSpec((TM, n), lambda i: (i, 0)), full((nstream, SUBLANES, n))],
        out_shape=[jax.ShapeDtypeStruct((m, n), out_dtype), jax.ShapeDtypeStruct((nstream, SUBLANES, n), F32)],
        scratch_shapes=[pltpu.VMEM((SUBLANES, n), F32)],
        compiler_params=_cp(),
        name=name,
    )(*([x, w] + ([wu] if gated else []) + [tails, conv_w, conv_b.reshape(1, n)]))


def _qkv_rope_body(x_ref, w_ref, cos_ref, sin_ref, q16_ref, k32_ref, k16_ref, v32_ref, v16_ref, vt_ref, *,
                   d_model, scale):
    cos = cos_ref[...]
    sin = sin_ref[...]
    lane = lax.broadcasted_iota(jnp.int32, cos.shape, 1)
    first_half = (lane % DA_HD) < (DA_HD // 2)
    x = x_ref[...].astype(BF16)
    wide = 2 * LANES
    for c0 in range(0, 3 * d_model, wide):
        acc = jnp.dot(x, w_ref[:, c0:c0 + wide], preferred_element_type=F32)
        for half in range(wide // LANES):
            col = c0 + half * LANES
            t = acc[:, half * LANES:(half + 1) * LANES]
            if col < 2 * d_model:
                partner = jnp.where(first_half, -pltpu.roll(t, LANES - DA_HD // 2, 1), pltpu.roll(t, DA_HD // 2, 1))
                out = t * cos + partner * sin
                if col < d_model:
                    q16_ref[:, col:col + LANES] = (out * scale).astype(BF16)
                else:
                    k32_ref[:, col - d_model:col - d_model + LANES] = out
                    k16_ref[:, col - d_model:col - d_model + LANES] = out.astype(BF16)
            else:
                vc = col - 2 * d_model
                v32_ref[:, vc:vc + LANES] = t
                v16_ref[:, vc:vc + LANES] = t.astype(BF16)
                vt_ref[vc // LANES, 0] = t.T.astype(BF16)


def _qkv_rope(x, w_qkv, cos, sin):
    m, d_model = x.shape
    t = ATT_TK
    nh = d_model // LANES
    rows = lambda dt: (pl.BlockSpec((t, d_model), lambda i: (i, 0)), jax.ShapeDtypeStruct((m, d_model), dt))
    outs = [rows(BF16), rows(F32), rows(BF16), rows(F32), rows(BF16),
            (pl.BlockSpec((nh, 1, LANES, t), lambda i: (0, i, 0, 0)),
             jax.ShapeDtypeStruct((nh, m // t, LANES, t), BF16))]
    return pl.pallas_call(
        functools.partial(_qkv_rope_body, d_model=d_model, scale=DA_HD ** -0.5 * math.log2(math.e)),
        grid=(m // t,),
        in_specs=[pl.BlockSpec((t, d_model), lambda i: (i, 0)),
                  pl.BlockSpec(w_qkv.shape, lambda i: (0, 0), pipeline_mode=pl.Buffered(1)),
                  pl.BlockSpec((t, LANES), lambda i: (i, 0)),
                  pl.BlockSpec((t, LANES), lambda i: (i, 0))],
        out_specs=[o[0] for o in outs],
        out_shape=[o[1] for o in outs],
        compiler_params=_cp(),
        name='da_qkv_rope',
    )(x, w_qkv, cos, sin)


def _diff_lambda(lam_ref, lam_init):
    lp = lam_ref[...]
    return (jnp.exp(jnp.sum(lp[0:1] * lp[1:2], axis=-1, keepdims=True))
            - jnp.exp(jnp.sum(lp[2:3] * lp[3:4], axis=-1, keepdims=True)) + lam_init)


def _attn_prompt_body(*refs, tq, lp, **kw):
    o_ref = refs[5]

    @pl.when(pl.program_id(1) < lp // tq)
    def _():
        _attn_prompt_tile(*refs, tq=tq, lp=lp, **kw)

    @pl.when(pl.program_id(1) >= lp // tq)
    def _():
        o_ref[...] = jnp.zeros_like(o_ref)


def _attn_prompt_tile(q_ref, k_ref, vt_ref, lam_ref, g_ref, o_ref, acc_ref, sa_ref, sb_ref, *, tq, tk, lp, lreal, off,
                      lam_init):
    qi = pl.program_id(1)
    nsub = tq // ATT_TQ_SUB
    chains = [(a, c) for a in range(nsub) for c in range(2)]
    qts, qchs = {}, {}
    for a in range(nsub):
        qt = q_ref[a * ATT_TQ_SUB:(a + 1) * ATT_TQ_SUB, :].astype(F32).T
        feat = lax.broadcasted_iota(jnp.int32, qt.shape, 0)
        qts[a, 0] = jnp.where(feat < DA_HD, qt, 0.0).astype(BF16)
        qts[a, 1] = jnp.where(feat >= DA_HD, qt, 0.0).astype(BF16)
        qpos = qi * tq + a * ATT_TQ_SUB + lax.broadcasted_iota(jnp.int32, (1, ATT_TQ_SUB), 1)
        qchs[a] = (qpos + off) >> 6
    acc_ref[...] = jnp.zeros_like(acc_ref)
    full_last = jnp.minimum(((qi * tq + off) >> 6) * CHUNK - off + CHUNK - 1, lreal - 1)
    any_last = jnp.minimum(((qi * tq + tq - 1 + off) >> 6) * CHUNK - off + CHUNK - 1, lreal - 1)
    nk = any_last // tk + 1
    nfull = jnp.minimum((full_last + 1) // tk, nk - 1)

    def scores(ki, s_ref, width=tk):
        kt = k_ref[pl.ds(pl.multiple_of(ki * tk, tk), width), :]
        for n, ch in enumerate(chains):
            s_ref[n, 0:width, :] = jnp.dot(kt, qts[ch], preferred_element_type=F32)

    def consume(ki, s_ref, carry, masked, width=tk):
        vt = vt_ref[0, ki, :, 0:width]
        s = [s_ref[n, 0:width, :] for n in range(len(chains))]
        if masked:
            kpos = ki * tk + lax.broadcasted_iota(jnp.int32, (width, 1), 0)
            for n, (a, _) in enumerate(chains):
                vis = jnp.logical_and(((kpos + off) >> 6) <= qchs[a], kpos < lreal)
                s[n] = jnp.where(vis, s[n], NEG_BIG)
        m_new = [jnp.maximum(carry[2 * n], jnp.max(s[n], axis=0, keepdims=True)) for n in range(len(chains))]
        p = [jnp.exp2(s[n] - m_new[n]) for n in range(len(chains))]
        new = []
        for n in range(len(chains)):
            alpha = jnp.exp2(carry[2 * n] - m_new[n])
            l_new = alpha * carry[2 * n + 1] + jnp.sum(p[n], axis=0, keepdims=True)
            acc_ref[n] = alpha * acc_ref[n] + jnp.dot(vt, p[n].astype(BF16), preferred_element_type=F32)
            new += [m_new[n], l_new]
        return tuple(new)

    npair = nfull // 2
    last_tile = nk - 1

    def pair(j, carry):
        scores(2 * j + 1, sb_ref)
        carry = consume(2 * j, sa_ref, carry, masked=False)
        scores(jnp.minimum(2 * j + 2, last_tile), sa_ref)
        return consume(2 * j + 1, sb_ref, carry, masked=False)

    def single(ki, carry, width=tk):
        scores(ki, sb_ref, width)
        return consume(ki, sb_ref, carry, True, width)

    carry = (jnp.full((1, ATT_TQ_SUB), NEG_BIG, F32), jnp.zeros((1, ATT_TQ_SUB), F32)) * len(chains)
    scores(0, sa_ref)
    carry = lax.fori_loop(0, npair, pair, carry)
    first_left = 2 * npair
    carry = lax.cond(first_left < last_tile, lambda c: consume(first_left, sa_ref, c, True), lambda c: c, carry)
    carry = lax.fori_loop(first_left + 1, last_tile, single, carry)
    narrow = any_last - last_tile * tk < ATT_TK_TAIL
    carry = lax.cond(narrow, functools.partial(single, last_tile, width=ATT_TK_TAIL),
                     functools.partial(single, last_tile), carry)
    lam = _diff_lambda(lam_ref, lam_init)
    for a in range(nsub):
        l0, l1 = carry[4 * a + 1], carry[4 * a + 3]
        o = acc_ref[2 * a] / l0 - lam * (acc_ref[2 * a + 1] / l1)
        ms = jnp.mean(o * o, axis=0, keepdims=True)
        o = o * lax.rsqrt(ms + SUBLN_EPS) * g_ref[...] * (1.0 - lam_init)
        o_ref[a * ATT_TQ_SUB:(a + 1) * ATT_TQ_SUB, :] = o.T.astype(o_ref.dtype)


def _attn_sample_body(q_ref, kn_ref, vn_ref, kc_ref, vc_ref, lam_ref, g_ref, o_alias_ref, o_ref, acc_ref, *,
                      tk, past, dec_seq, lam_init):
    del o_alias_ref
    nq = -(-dec_seq // 16) * 16
    q = q_ref[0:nq, :]
    lane = lax.broadcasted_iota(jnp.int32, q.shape, 1)
    zero = jnp.zeros_like(q)
    q2 = jnp.concatenate([jnp.where(lane < DA_HD, q, zero), jnp.where(lane >= DA_HD, q, zero)], axis=0)
    acc_ref[...] = jnp.zeros_like(acc_ref)
    qrow = lax.broadcasted_iota(jnp.int32, (2 * nq, 1), 0)
    qch = (past + jnp.where(qrow >= nq, qrow - nq, qrow)) >> 6

    def tile(kt, vt, vis, carry):
        m_old, l_old = carry
        s = lax.dot_general(q2, kt, _NT, preferred_element_type=F32)
        s = jnp.where(vis, s, NEG_BIG)
        m_new = jnp.maximum(m_old, jnp.max(s, axis=-1, keepdims=True))
        p = jnp.exp2(s - m_new)
        alpha = jnp.exp2(m_old - m_new)
        l_new = alpha * l_old + jnp.sum(p, axis=-1, keepdims=True)
        acc_ref[...] = alpha * acc_ref[...] + jnp.dot(p.astype(BF16), vt, preferred_element_type=F32)
        return m_new, l_new

    def body(ki, c):
        ks = pl.multiple_of(ki * tk, tk)
        kt = kc_ref[0, pl.ds(ks, tk), :].astype(BF16)
        vt = vc_ref[0, pl.ds(ks, tk), :].astype(BF16)
        kpos = ks + lax.broadcasted_iota(jnp.int32, (1, tk), 1)
        return tile(kt, vt, (kpos >> 6) <= qch, c)

    carry = (jnp.full((2 * nq, 1), NEG_BIG, F32), jnp.zeros((2 * nq, 1), F32))
    carry = lax.fori_loop(0, past // tk, body, carry)
    kidx = lax.broadcasted_iota(jnp.int32, (1, nq), 1)
    vis = jnp.logical_and(((past + kidx) >> 6) <= qch, kidx < dec_seq)
    _, l = tile(kn_ref[0:nq, :], vn_ref[0:nq, :], vis, carry)
    lam = _diff_lambda(lam_ref, lam_init)
    on = acc_ref[...] / l
    o = on[:nq] - lam * on[nq:]
    ms = jnp.mean(o * o, axis=-1, keepdims=True)
    o_ref[...] = jnp.zeros_like(o_ref)
    o_ref[0:nq, :] = (o * lax.rsqrt(ms + SUBLN_EPS) * g_ref[...] * (1.0 - lam_init)).astype(o_ref.dtype)


def _attention(q16, k16, v16, vt16, k_cache, v_cache, lam_p, subln_g, lam_init, lay):
    m, d = q16.shape
    nh = d // LANES
    lp = lay['lp']
    ns = lay['ns']
    past = k_cache.shape[1]
    g = subln_g.astype(F32)
    lam_p = lam_p.astype(F32)
    nkt = vt16.shape[1]
    o_prompt = pl.pallas_call(
        functools.partial(_attn_prompt_body, tq=ATT_TQ, tk=ATT_TK, lp=lp, lreal=lay['lreal'], off=CHUNK - N_META,
                          lam_init=lam_init),
        grid=(nh, m // ATT_TQ),
        in_specs=[pl.BlockSpec((ATT_TQ, LANES), lambda h, i: (i, h)),
                  pl.BlockSpec((lp, LANES), lambda h, i: (0, h)),
                  pl.BlockSpec((1, nkt, LANES, ATT_TK), lambda h, i: (h, 0, 0, 0)),
                  pl.BlockSpec((4, DA_HD), lambda h, i: (0, 0)),
                  pl.BlockSpec((LANES, 1), lambda h, i: (0, 0))],
        out_specs=pl.BlockSpec((ATT_TQ, LANES), lambda h, i: (i, h)),
        out_shape=jax.ShapeDtypeStruct((m, d), BF16),
        scratch_shapes=[pltpu.VMEM((2 * ATT_TQ // ATT_TQ_SUB, LANES, ATT_TQ_SUB), F32),
                        pltpu.VMEM((2 * ATT_TQ // ATT_TQ_SUB, ATT_TK, ATT_TQ_SUB), F32),
                        pltpu.VMEM((2 * ATT_TQ // ATT_TQ_SUB, ATT_TK, ATT_TQ_SUB), F32)],
        compiler_params=_cp(),
        name='attn_prompt',
    )(q16, k16, vt16, lam_p, g.reshape(LANES, 1))
    nbp = lay['nbp']
    tk = min(ATT_TK_SAMPLE, past)
    nq = -(-lay['dec_seq'] // 16) * 16
    blk_spec = pl.BlockSpec((BLK, LANES), lambda s, h: (nbp + s, h))
    return pl.pallas_call(
        functools.partial(_attn_sample_body, tk=tk, past=past, dec_seq=lay['dec_seq'], lam_init=lam_init),
        grid=(ns, nh),
        in_specs=[blk_spec, blk_spec, blk_spec,
                  pl.BlockSpec((1, past, LANES), lambda s, h: (s, 0, h)),
                  pl.BlockSpec((1, past, LANES), lambda s, h: (s, 0, h)),
                  pl.BlockSpec((4, DA_HD), lambda s, h: (0, 0)),
                  pl.BlockSpec((1, LANES), lambda s, h: (0, 0)),
                  pl.BlockSpec(memory_space=pl.ANY)],
        out_specs=blk_spec,
        out_shape=jax.ShapeDtypeStruct((m, d), BF16),
        scratch_shapes=[pltpu.VMEM((2 * nq, LANES), F32)],
        input_output_aliases={7: 0},
        compiler_params=_cp(),
        name='attn_sample',
    )(q16, k16, v16, k_cache, v_cache, lam_p, g.reshape(1, LANES), o_prompt)


def _rk_in_body(x_ref, tail_ref, mu_ref, wr_ref, wk_ref, wv_ref, w1_ref, w2_ref, a1_ref, a2_ref, g1_ref, g2_ref,
                r_ref, k_ref, v_ref, lw_ref, la_ref, g_ref, carry_ref, *, nbp, lreal, dec_seq, tm):
    i = pl.program_id(0)

    @pl.when(i == 0)
    def _():
        carry_ref[...] = jnp.zeros_like(carry_ref)

    x = x_ref[...]
    shifted = []
    for j in range(tm // BLK):
        blk = i * (tm // BLK) + j
        is_start, _ = _block_info(blk, nbp, lreal, dec_seq)
        sid = jnp.maximum(blk - nbp + 1, 0)
        before = carry_ref[...] if j == 0 else x[j * BLK - SUBLANES:j * BLK]
        prev8 = jnp.where(is_start, tail_ref[sid], before)
        ext = jnp.concatenate([prev8, x[j * BLK:(j + 1) * BLK]], axis=0)
        shifted.append(ext[SUBLANES - 1:SUBLANES - 1 + BLK])
    carry_ref[...] = x[tm - SUBLANES:tm]
    xx = jnp.concatenate(shifted, axis=0) - x

    def mix(n):
        return (x + xx * mu_ref[n:n + 1, :]).astype(BF16)

    def mm(a, w_ref):
        return jnp.dot(a, w_ref[...], preferred_element_type=F32)

    r_ref[...] = mm(mix(0), wr_ref)
    lw_ref[...] = mm(jnp.tanh(mm(mix(1), w1_ref)).astype(BF16), w2_ref)
    k_ref[...] = mm(mix(2), wk_ref)
    v_ref[...] = mm(mix(3), wv_ref)
    la_ref[...] = mm(mm(mix(4), a1_ref).astype(BF16), a2_ref)
    g_ref[...] = mm(_sigmoid(mm(mix(5), g1_ref)).astype(BF16), g2_ref)


def _rk_in(x, tails, mu, w_r, w_k, w_v, w1, w2, a1, a2, g1, g2, lay):
    m, d = x.shape
    nstream = tails.shape[0]

    def lora_pair(u, w):
        r = u.shape[1]
        rp = -(-r // LANES) * LANES
        return jnp.pad(u, ((0, 0), (0, rp - r))).astype(BF16), jnp.pad(w, ((0, rp - r), (0, 0))).astype(BF16)

    ws = [w_r.astype(BF16), w_k.astype(BF16), w_v.astype(BF16), *lora_pair(w1, w2), *lora_pair(a1, a2),
          *lora_pair(g1, g2)]
    full = lambda a: pl.BlockSpec(a.shape, lambda i: (0,) * a.ndim, pipeline_mode=pl.Buffered(1))
    row = pl.BlockSpec((TM, d), lambda i: (i, 0))
    body = functools.partial(_rk_in_body, nbp=lay['nbp'], lreal=lay['lreal'], dec_seq=lay['dec_seq'], tm=TM)
    return pl.pallas_call(
        body,
        grid=(m // TM,),
        in_specs=[row, full(tails), full(mu)] + [full(w) for w in ws],
        out_specs=[row] * 6,
        out_shape=[jax.ShapeDtypeStruct((m, d), F32)] * 6,
        scratch_shapes=[pltpu.VMEM((SUBLANES, d), F32)],
        compiler_params=_cp(),
        name='rk_in',
    )(x, tails, mu, *ws)


def _half_sum(x, low):
    s_lo = jnp.sum(jnp.where(low, x, 0.0), axis=-1, keepdims=True)
    s_hi = jnp.sum(jnp.where(low, 0.0, x), axis=-1, keepdims=True)
    return jnp.where(low, s_lo, s_hi)


def _wkv_body(r_ref, k_ref, v_ref, lw_ref, la_ref, g_ref, s0_ref,
              w0_ref, a0_ref, kk_ref, ka_ref, lng_ref, lnb_ref, rk_ref,
              o_ref, st_ref, stage_ref, yb_ref, pw_ref, x_ref, ak_ref, dd_ref, es_ref, bke_ref, gc_ref,
              *, nbp, lreal, dec_seq):
    c = BLK
    npair = st_ref.shape[1]
    half = c // 2
    b = pl.program_id(0)
    is_start, nvalid = _block_info(b, nbp, lreal, dec_seq)

    @pl.when(is_start)
    def _():
        st_ref[0] = s0_ref[0]

    for i, ref in enumerate((r_ref, k_ref, v_ref, lw_ref, la_ref, g_ref)):
        for p in range(npair):
            stage_ref[i, p] = ref[:, p * LANES:(p + 1) * LANES]

    row = lax.broadcasted_iota(jnp.int32, (c, 1), 0)
    valid = row < nvalid
    lane = lax.broadcasted_iota(jnp.int32, (c, LANES), 1)
    low = lane < RK_HD
    ri = lax.broadcasted_iota(jnp.int32, (c, c), 0)
    ci = lax.broadcasted_iota(jnp.int32, (c, c), 1)
    lower_incl = ri >= ci
    lower_strict = ri > ci
    tri = lower_incl.astype(F32)
    eye = (ri == ci).astype(F32)
    same_head = (ri < RK_HD) == (ci < RK_HD)

    prep_group = 4

    def each(f, *cols):
        return [f(*args) for args in zip(*cols)]

    def prep(g, carry):
        ps = [g * prep_group + i for i in range(prep_group)]
        r = [stage_ref[0, p] for p in ps]
        k = [stage_ref[1, p] for p in ps]
        v = [jnp.where(valid, stage_ref[2, p], 0.0) for p in ps]
        w_log = [-_softplus(-(w0_ref[p] + stage_ref[3, p])) - 0.5 for p in ps]
        logw = each(lambda t: jnp.where(valid, -jnp.exp(t), 0.0), w_log)
        ag = [_sigmoid(a0_ref[p] + stage_ref[4, p]) for p in ps]
        kk = [ki * kk_ref[p] for ki, p in zip(k, ps)]
        kk = each(lambda t: t / jnp.maximum(jnp.sqrt(_half_sum(t * t, low)), 1e-12), kk)
        kxm = [jnp.where(valid, ki * (1.0 + (gi - 1.0) * ka_ref[p]), 0.0) for ki, gi, p in zip(k, ag, ps)]
        a = each(lambda t: jnp.where(valid, -t, 0.0), kk)
        bv = each(lambda t, gi: jnp.where(valid, t * gi, 0.0), kk, ag)

        cum = each(lambda t: jnp.dot(tri, t, preferred_element_type=F32, precision=lax.Precision.HIGHEST), logw)
        e_cum = each(jnp.exp, cum)
        e_prev = each(lambda ci_, li: jnp.exp(ci_ - li), cum, logw)
        e_neg = each(lambda ci_: jnp.exp(-ci_), cum)
        mid_dn = each(lambda ci_: jnp.exp(-ci_[half - 1:half, :]), cum)
        mid_up = each(lambda ci_: jnp.exp(ci_[half - 1:half, :]), cum)
        end_up = each(lambda ci_: jnp.exp(ci_[c - 1:c, :]), cum)
        a_abs = each(lambda x_, e: x_ * e, a, e_prev)
        r_abs = each(lambda x_, e: x_ * e, r, e_cum)
        bn = each(lambda x_, e: x_ * e, bv, e_neg)
        kn = each(lambda x_, e: x_ * e, kxm, e_neg)

        for i, p in enumerate(ps):
            stage_ref[1, p] = kxm[i]
            stage_ref[2, p] = v[i]
            bke_ref[p] = jnp.concatenate([bn[i] * end_up[i], kn[i] * end_up[i]], axis=0).astype(BF16)
            gc_ref[p] = jnp.broadcast_to(end_up[i], (SUBLANES, LANES))
        for i, p in enumerate(ps):
            s = st_ref[0, p]
            es_ref[p] = _bdot_nt(jnp.concatenate([a_abs[i], r_abs[i]], axis=0), s)
        for i, p in enumerate(ps):
            a_rel = a_abs[i] * mid_dn[i]
            r_rel = r_abs[i] * mid_dn[i]
            bk_rel = jnp.concatenate([bn[i] * mid_up[i], kn[i] * mid_up[i]], axis=0).astype(BF16)
            for j, sel in enumerate((low, jnp.logical_not(low))):
                ar = jnp.concatenate([jnp.where(sel, a_rel, 0.0), jnp.where(sel, r_rel, 0.0)], axis=0)
                mm = lax.dot_general(ar.astype(BF16), bk_rel, _NT, preferred_element_type=F32)
                n = jnp.where(lower_strict, mm[:c, :c], 0.0)
                h = 2 * p + j
                pw_ref[h] = n.astype(BF16)
                x_ref[h] = eye + n
                ak_ref[h] = jnp.where(lower_strict, mm[:c, c:], 0.0).astype(BF16)
                dd_ref[h] = jnp.concatenate([jnp.where(lower_incl, mm[c:, :c], 0.0),
                                             jnp.where(lower_incl, mm[c:, c:], 0.0)], axis=1).astype(BF16)
        return carry

    lax.fori_loop(0, npair // prep_group, prep, 0)

    nlev = int(math.log2(c))
    inv_group = 8

    def invert(g, carry):
        hs = [g * inv_group + j for j in range(inv_group)]
        for h in hs:
            pw = pw_ref[h]
            pw_ref[h] = jnp.dot(pw, pw, preferred_element_type=F32).astype(BF16)
        for _ in range(1, nlev - 1):
            for h in hs:
                pw = pw_ref[h]
                both = jnp.dot(jnp.concatenate([pw, x_ref[h].astype(BF16)], axis=0), pw,
                               preferred_element_type=F32)
                pw_ref[h] = both[:c].astype(BF16)
                x_ref[h] = x_ref[h] + both[c:]
        for h in hs:
            x = x_ref[h]
            x_ref[h] = x + jnp.dot(x.astype(BF16), pw_ref[h], preferred_element_type=F32)
        return carry

    lax.fori_loop(0, 2 * npair // inv_group, invert, 0)

    out_group = 4

    def emit(g, carry):
        ps = [g * out_group + i for i in range(out_group)]
        hs = [(i, j) for i in range(out_group) for j in range(2)]
        rhs, us, ys = {}, {}, {}
        for i, j in hs:
            p = ps[i]
            rhs[i, j] = es_ref[p, 0:c, :] + jnp.dot(ak_ref[2 * p + j], stage_ref[2, p].astype(BF16),
                                                    preferred_element_type=F32)
        for i, j in hs:
            us[i, j] = _bdot(x_ref[2 * ps[i] + j], rhs[i, j])
        for i, j in hs:
            p = ps[i]
            uv = jnp.concatenate([us[i, j], stage_ref[2, p]], axis=0).astype(BF16)
            ys[i, j] = es_ref[p, c:2 * c, :] + jnp.dot(dd_ref[2 * p + j], uv, preferred_element_type=F32)
        for i, p in enumerate(ps):
            v = stage_ref[2, p]
            u = jnp.where(low, us[i, 0], us[i, 1])
            y = jnp.where(low, ys[i, 0], ys[i, 1])
            uv_t = jnp.concatenate([u, v], axis=0).T.astype(BF16)
            upd = jnp.dot(uv_t, bke_ref[p], preferred_element_type=F32)
            st_ref[0, p] = st_ref[0, p] * gc_ref[p, 0:1, :] + jnp.where(same_head, upd, 0.0)

            mean = _half_sum(y, low) * (1.0 / RK_HD)
            dy = y - mean
            var = _half_sum(dy * dy, low) * (1.0 / RK_HD)
            yn = dy * lax.rsqrt(var + RK_GN_EPS) * lng_ref[p] + lnb_ref[p]
            bonus = _half_sum(stage_ref[0, p] * stage_ref[1, p] * rk_ref[p], low) * v
            yb_ref[p] = ((yn + bonus) * stage_ref[5, p]).astype(yb_ref.dtype)
        return carry

    lax.fori_loop(0, npair // out_group, emit, 0)
    for p in range(npair):
        o_ref[:, p * LANES:(p + 1) * LANES] = yb_ref[p]


def _wkv(r, k, v, lw, la, gate, s0, w0, a0, k_k, k_a, lnx_g, lnx_b, r_k, lay):
    m, d = r.shape
    npair = d // LANES

    def prm(t):
        return t.astype(F32).reshape(npair, 1, LANES)

    body = functools.partial(_wkv_body, nbp=lay['nbp'], lreal=lay['lreal'], dec_seq=lay['dec_seq'])
    row_spec = pl.BlockSpec((BLK, d), lambda b: (b, 0))
    prm_spec = pl.BlockSpec((npair, 1, LANES), lambda b: (0, 0, 0))
    smap = _stream_map(lay['nbp'], 4)
    return pl.pallas_call(
        body,
        grid=(m // BLK,),
        in_specs=[row_spec] * 6 + [pl.BlockSpec((1,) + s0.shape[1:], smap)] + [prm_spec] * 7,
        out_specs=[row_spec, pl.BlockSpec((1,) + s0.shape[1:], smap)],
        out_shape=[jax.ShapeDtypeStruct((m, d), BF16), jax.ShapeDtypeStruct(s0.shape, F32)],
        scratch_shapes=[pltpu.VMEM((6, npair, BLK, LANES), F32),
                        pltpu.VMEM((npair, BLK, LANES), BF16),
                        pltpu.VMEM((2 * npair, BLK, BLK), BF16),
                        pltpu.VMEM((2 * npair, BLK, BLK), F32),
                        pltpu.VMEM((2 * npair, BLK, BLK), BF16),
                        pltpu.VMEM((2 * npair, BLK, 2 * BLK), BF16),
                        pltpu.VMEM((npair, 2 * BLK, LANES), F32),
                        pltpu.VMEM((npair, 2 * BLK, LANES), BF16),
                        pltpu.VMEM((npair, SUBLANES, LANES), F32)],
        compiler_params=_cp(),
        name='wkv7',
    )(r, k, v, lw, la, gate, s0, prm(w0), prm(a0), prm(k_k), prm(k_a), prm(lnx_g), prm(lnx_b), prm(r_k))


def _tails(prev, nstream):
    ns, w, c = prev.shape
    t = jnp.zeros((nstream, SUBLANES, c), F32)
    return t.at[1:, SUBLANES - w:, :].set(prev.astype(F32))


def _last_rows(flat, w, lay):
    lreal, lp, ns, dec = lay['lreal'], lay['lp'], lay['ns'], lay['dec_seq']
    p = flat[lreal - w:lreal][None]
    s = flat[lp:].reshape(ns, BLK, -1)[:, dec - w:dec]
    return p, s


def kernel(x_prompt, x_sample, cache_attn_k, cache_attn_v, state_ssm, state_ssd_conv, state_wkv, state_rwkv_shift, state_ffn_conv, meta_tokens, ssd_w_in, ssd_conv_w, ssd_conv_b, ssd_dt_bias, ssd_a_log, ssd_d, ssd_norm_w, ssd_w_out, da_w_qkv, da_lambda, da_subln_g, da_w_o, rk_mu, rk_w0, rk_w1, rk_w2, rk_a0, rk_a1, rk_a2, rk_g1, rk_g2, rk_k_k, rk_k_a, rk_r_k, rk_w_r, rk_w_k, rk_w_v, rk_w_o, rk_lnx_g, rk_lnx_b, ffn_w_up, ffn_w_gate, ffn_conv_w, ffn_conv_b, ffn_w_down, ln_g, ln_b):
    bp, seq, d = x_prompt.shape
    ns, dec_seq, _ = x_sample.shape
    depth = ln_g.shape[0]
    past = cache_attn_k.shape[2]
    assert bp == 1 and dec_seq >= SSD_CONV - 1 and dec_seq <= BLK and past % min(ATT_TK_SAMPLE, past) == 0
    lreal = N_META + seq
    row_align = max(TM, ATT_TQ, ATT_TK)
    lp = -(-lreal // row_align) * row_align
    m = lp + ns * BLK
    assert m % row_align == 0
    nstream = ns + 1
    lay = dict(lreal=lreal, lp=lp, nbp=lp // BLK, ns=ns, dec_seq=dec_seq)
    alpha = (2 * depth) ** 0.25

    x = jnp.concatenate([
        meta_tokens.astype(F32), x_prompt[0], jnp.zeros((lp - lreal, d), F32),
        jnp.pad(x_sample, ((0, 0), (0, BLK - dec_seq), (0, 0))).reshape(ns * BLK, d)], axis=0)

    pos = jnp.concatenate([jnp.arange(lp), jnp.tile(past + jnp.arange(BLK), ns)]).astype(F32)
    half = DA_HD // 2
    inv = ROPE_THETA ** (-jnp.arange(half, dtype=F32) / half)
    ang = pos[:, None] * inv[None, :]
    cos_t = jnp.tile(jnp.cos(ang), (1, LANES // half))
    sin_t = jnp.tile(jnp.sin(ang), (1, LANES // half))

    inner = ssd_norm_w.shape[1]
    nheads = ssd_dt_bias.shape[1]
    hpg = nheads // SSD_GROUPS
    new = {n: [] for n in ('k', 'v', 'ssm', 'ssd_conv', 'wkv', 'shift', 'ffn_conv')}

    for i in range(depth):
        j, kind = i // N_MIXERS, i % N_MIXERS
        if kind == 0:
            w_in = ssd_w_in[j]
            cdim = ssd_conv_w.shape[2]
            w_z = w_in[:, :inner].astype(BF16)
            w_xbc = w_in[:, inner:inner + cdim].astype(BF16)
            w_dt = jnp.pad(w_in[:, inner + cdim:], ((0, 0), (0, LANES - nheads))).astype(BF16)
            z, dt = _mm_multi(x, [w_z, w_dt], [F32, F32], 'ssd_in_z_dt')
            xc, conv_rows = _mm_conv(x, w_xbc, None, _tails(state_ssd_conv[j], nstream), ssd_conv_w[j],
                                     ssd_conv_b[j], F32, lay, 'ssd_in_xbc_conv')
            s0 = jnp.concatenate([jnp.zeros((1,) + state_ssm.shape[2:], F32), state_ssm[j].astype(F32)], axis=0)
            s0t = s0.reshape(nstream, SSD_GROUPS, hpg * SSD_HEADDIM, SSD_STATE).swapaxes(2, 3)
            y, st = _ssd_scan(xc, z, dt, s0t, ssd_dt_bias[j], ssd_a_log[j], ssd_d[j], ssd_norm_w[j], lay)
            w_o = ssd_w_out[j].astype(BF16)
            conv_rows = conv_rows[:, SUBLANES - (SSD_CONV - 1):]
            new['ssd_conv'].append((conv_rows[:1], conv_rows[1:]))
            ssm = st.swapaxes(2, 3).reshape(nstream, nheads, SSD_HEADDIM, SSD_STATE)
            new['ssm'].append((ssm[:1], ssm[1:]))
        elif kind == 1:
            lam_init = 0.8 - 0.6 * math.exp(-0.3 * i)
            q16, k32, k16, v32, v16, vt16 = _qkv_rope(x, da_w_qkv[j].astype(BF16), cos_t, sin_t)
            nh = d // (2 * DA_HD)
            y = _attention(q16, k16, v16, vt16, cache_attn_k[j].reshape(ns, past, d),
                           cache_attn_v[j].reshape(ns, past, d), da_lambda[j], da_subln_g[j], lam_init, lay)
            w_o = da_w_o[j].astype(BF16)
            kp, ks = k32[:lreal][None], k32[lp:].reshape(ns, BLK, d)[:, :dec_seq]
            vp, vs = v32[:lreal][None], v32[lp:].reshape(ns, BLK, d)[:, :dec_seq]
            new['k'].append((kp.reshape(1, lreal, nh, 2, DA_HD), ks.reshape(ns, dec_seq, nh, 2, DA_HD)))
            new['v'].append((vp.reshape(1, lreal, nh, 2 * DA_HD), vs.reshape(ns, dec_seq, nh, 2 * DA_HD)))
        else:
            r, k, v, lw, la, gate = _rk_in(x, _tails(state_rwkv_shift[j], nstream), rk_mu[j], rk_w_r[j], rk_w_k[j],
                                           rk_w_v[j], rk_w1[j], rk_w2[j], rk_a1[j], rk_a2[j], rk_g1[j], rk_g2[j], lay)
            nrh = d // RK_HD
            npair = nrh // 2
            sw = jnp.concatenate([jnp.zeros((1,) + state_wkv.shape[2:], F32), state_wkv[j].astype(F32)], axis=0)
            sw = sw.reshape(nstream, npair, 2, RK_HD, RK_HD)
            zero = jnp.zeros_like(sw[:, :, 0])
            s0 = jnp.concatenate([jnp.concatenate([sw[:, :, 0], zero], axis=-1),
                                  jnp.concatenate([zero, sw[:, :, 1]], axis=-1)], axis=-2)
            y, st = _wkv(r, k, v, lw, la, gate, s0, rk_w0[j], rk_a0[j], rk_k_k[j], rk_k_a[j], rk_lnx_g[j],
                         rk_lnx_b[j], rk_r_k[j].reshape(-1), lay)
            w_o = rk_w_o[j].astype(BF16)
            wkv = jnp.stack([st[:, :, :RK_HD, :RK_HD], st[:, :, RK_HD:, RK_HD:]], axis=2).reshape(
                nstream, nrh, RK_HD, RK_HD)
            new['wkv'].append((wkv[:1], wkv[1:]))
            new['shift'].append(_last_rows(x, 1, lay))
        x = _mm_ln(y, w_o, x, ln_g[i, 0], ln_b[i, 0], alpha, 'mixer_out_ln')
        h, gate_rows = _mm_conv(x, ffn_w_gate[i].astype(BF16), ffn_w_up[i].astype(BF16),
                                _tails(state_ffn_conv[i], nstream), ffn_conv_w[i], ffn_conv_b[i], BF16, lay, 'ffn_in_conv')
        gate_rows = gate_rows[:, SUBLANES - (FFN_CONV - 1):]
        new['ffn_conv'].append((gate_rows[:1], gate_rows[1:]))
        x = _mm_ln(h, ffn_w_down[i].astype(BF16), x, ln_g[i, 1], ln_b[i, 1], alpha, 'ffn_out_ln')

    y_prompt = x[N_META:lreal][None]
    y_sample = x[lp:].reshape(ns, BLK, d)[:, :dec_seq]
    order = ('k', 'v', 'ssm', 'ssd_conv', 'wkv', 'shift', 'ffn_conv')
    prompt_states = tuple(jnp.stack([t[0] for t in new[n]]) for n in order)
    sample_states = tuple(jnp.stack([t[1] for t in new[n]]) for n in order)
    return (y_prompt, y_sample) + prompt_states + sample_states
```

```python
import functools
import math

import jax
import jax.numpy as jnp
from jax import lax
from jax.experimental import pallas as pl
from jax.experimental.pallas import tpu as pltpu

F32 = jnp.float32
BF16 = jnp.bfloat16

CHUNK = 64
N_META = 16
N_MIXERS = 3
SSD_HEADDIM = 64
SSD_GROUPS = 8
SSD_STATE = 128
SSD_CONV = 4
SSD_NORM_EPS = 1e-5
DA_HD = 64
ROPE_THETA = 10000.0
SUBLN_EPS = 1e-5
RK_HD = 64
RK_GN_EPS = 64e-5
FFN_CONV = 3
LN_EPS = 1e-5

LANES = 128
SUBLANES = 8
BLK = 128
TM = 512
ATT_TQ = 512
ATT_TQ_SUB = 256
ATT_TK = 512
ATT_TK_TAIL = 128
ATT_TK_SAMPLE = 4096
VMEM_LIMIT = 56 * 1024 * 1024
NEG_BIG = -1e30

_NT = (((1,), (1,)), ((), ()))


def _cp():
    return pltpu.CompilerParams(vmem_limit_bytes=VMEM_LIMIT)


def _sigmoid(x):
    return 0.5 + 0.5 * jnp.tanh(0.5 * x)


def _softplus(x):
    return jnp.maximum(x, 0.0) + jnp.log(1.0 + jnp.exp(-jnp.abs(x)))


def _bdot(a, b):
    return jnp.dot(a.astype(BF16), b.astype(BF16), preferred_element_type=F32)


def _bdot_nt(a, b):
    return lax.dot_general(a.astype(BF16), b.astype(BF16), _NT, preferred_element_type=F32)


def _mm_multi_body(*refs, n):
    x = refs[0][...].astype(BF16)
    for i in range(n):
        refs[1 + n + i][...] = jnp.dot(x, refs[1 + i][...], preferred_element_type=F32).astype(refs[1 + n + i].dtype)


def _mm_multi(x, ws, out_dtypes, name):
    m, k = x.shape
    n = len(ws)
    outs = pl.pallas_call(
        functools.partial(_mm_multi_body, n=n),
        grid=(m // TM,),
        in_specs=[pl.BlockSpec((TM, k), lambda i: (i, 0))]
        + [pl.BlockSpec(w.shape, lambda i: (0, 0), pipeline_mode=pl.Buffered(1)) for w in ws],
        out_specs=[pl.BlockSpec((TM, w.shape[1]), lambda i: (i, 0)) for w in ws],
        out_shape=[jax.ShapeDtypeStruct((m, w.shape[1]), dt) for w, dt in zip(ws, out_dtypes)],
        compiler_params=_cp(),
        name=name,
    )(x, *ws)
    return outs


def _mm_ln_body(x_ref, w_ref, r_ref, g_ref, b_ref, o_ref, *, alpha):
    acc = jnp.dot(x_ref[...].astype(BF16), w_ref[...], preferred_element_type=F32)
    h = alpha * r_ref[...] + acc
    mu = jnp.mean(h, axis=-1, keepdims=True)
    d = h - mu
    var = jnp.mean(d * d, axis=-1, keepdims=True)
    o_ref[...] = d * lax.rsqrt(var + LN_EPS) * g_ref[...] + b_ref[...]


def _mm_ln(x, w, resid, g, b, alpha, name):
    m, k = x.shape
    d = w.shape[1]
    return pl.pallas_call(
        functools.partial(_mm_ln_body, alpha=alpha),
        grid=(m // TM,),
        in_specs=[pl.BlockSpec((TM, k), lambda i: (i, 0)),
                  pl.BlockSpec((k, d), lambda i: (0, 0), pipeline_mode=pl.Buffered(1)),
                  pl.BlockSpec((TM, d), lambda i: (i, 0)),
                  pl.BlockSpec((1, d), lambda i: (0, 0)),
                  pl.BlockSpec((1, d), lambda i: (0, 0))],
        out_specs=pl.BlockSpec((TM, d), lambda i: (i, 0)),
        out_shape=jax.ShapeDtypeStruct((m, d), F32),
        compiler_params=_cp(),
        name=name,
    )(x, w, resid, g.reshape(1, d), b.reshape(1, d))


def _block_info(b, nbp, lreal, dec_seq):
    is_start = jnp.logical_or(b == 0, b >= nbp)
    nvalid = jnp.where(b < nbp, jnp.clip(lreal - b * BLK, 0, BLK), dec_seq)
    return is_start, nvalid


def _stream_map(nbp, ndim):
    def index_map(b):
        return (jnp.maximum(b - nbp + 1, 0),) + (0,) * (ndim - 1)
    return index_map


def _expand_heads(colmat, g, lane_group):
    out = colmat[:, 4 * g + 3:4 * g + 4]
    for r in (2, 1, 0):
        out = jnp.where(lane_group == r, colmat[:, 4 * g + r:4 * g + r + 1], out)
    return out


def _ssd_body(xc_ref, z_ref, dt_ref, s0_ref, dtb_ref, alog_ref, d_ref, nw_ref,
              y_ref, st_ref, yb_ref, *, nbp, lreal, dec_seq):
    q = BLK
    inner = z_ref.shape[1]
    gn = SSD_GROUPS * SSD_STATE
    b = pl.program_id(0)
    is_start, nvalid = _block_info(b, nbp, lreal, dec_seq)

    @pl.when(is_start)
    def _():
        st_ref[0] = s0_ref[0]

    row = lax.broadcasted_iota(jnp.int32, (q, 1), 0)
    valid = row < nvalid
    dt = _softplus(dt_ref[...] + dtb_ref[...])
    dt = jnp.where(valid, dt, 0.0)
    a = -jnp.exp(alog_ref[...])
    adt = dt * a
    ri = lax.broadcasted_iota(jnp.int32, (q, q), 0)
    ci = lax.broadcasted_iota(jnp.int32, (q, q), 1)
    causal = ri >= ci
    tri = causal.astype(F32)
    acs = jnp.dot(tri, adt, preferred_element_type=F32, precision=lax.Precision.HIGHEST)
    acs_t = acs.T
    last = acs[q - 1:q, :]
    lane_group = lax.broadcasted_iota(jnp.int32, (1, 4 * SSD_HEADDIM), 1) // SSD_HEADDIM

    ssq = jnp.zeros((q, 1), F32)
    for g in range(SSD_GROUPS):
        bb = xc_ref[:, inner + g * SSD_STATE:inner + (g + 1) * SSD_STATE]
        cc = xc_ref[:, inner + gn + g * SSD_STATE:inner + gn + (g + 1) * SSD_STATE]
        xs = xc_ref[:, g * 256:(g + 1) * 256]
        cbm = jnp.where(causal, _bdot_nt(cc, bb), 0.0)
        dt_x = _expand_heads(dt, g, lane_group)
        acs_x = _expand_heads(acs, g, lane_group)
        last_x = _expand_heads(last, g, lane_group)
        xdt = xs * dt_x
        st = st_ref[0, g]
        y = _bdot(cc, st) * jnp.exp(acs_x)
        for r in range(4):
            h = 4 * g + r
            seg = acs[:, h:h + 1] - acs_t[h:h + 1, :]
            att = cbm * jnp.exp(jnp.minimum(seg, 0.0))
            y = y + _bdot(att, jnp.where(lane_group == r, xdt, 0.0))
        xdtd = xdt * jnp.exp(last_x - acs_x)
        st_ref[0, g] = st * jnp.exp(last_x) + _bdot(bb.T, xdtd)
        y = y + _expand_heads(d_ref[...], g, lane_group) * xs
        zz = z_ref[:, g * 256:(g + 1) * 256]
        y = y * (zz * _sigmoid(zz))
        ssq = ssq + jnp.sum(y * y, axis=-1, keepdims=True)
        yb_ref[:, g * 256:(g + 1) * 256] = y
    scale = lax.rsqrt(ssq / inner + SSD_NORM_EPS)
    y_ref[...] = (yb_ref[...] * scale * nw_ref[...]).astype(y_ref.dtype)


def _ssd_scan(xc, z, dt, s0t, dt_bias, a_log, d_skip, norm_w, lay):
    m, cdim = xc.shape
    inner = z.shape[1]
    nh = dt_bias.shape[0]

    def pad_h(v):
        return jnp.pad(v.astype(F32), (0, LANES - nh)).reshape(1, LANES)

    body = functools.partial(_ssd_body, nbp=lay['nbp'], lreal=lay['lreal'], dec_seq=lay['dec_seq'])
    smap4 = _stream_map(lay['nbp'], 4)
    return pl.pallas_call(
        body,
        grid=(m // BLK,),
        in_specs=[pl.BlockSpec((BLK, cdim), lambda b: (b, 0)),
                  pl.BlockSpec((BLK, inner), lambda b: (b, 0)),
                  pl.BlockSpec((BLK, LANES), lambda b: (b, 0)),
                  pl.BlockSpec((1,) + s0t.shape[1:], smap4),
                  pl.BlockSpec((1, LANES), lambda b: (0, 0)),
                  pl.BlockSpec((1, LANES), lambda b: (0, 0)),
                  pl.BlockSpec((1, LANES), lambda b: (0, 0)),
                  pl.BlockSpec((1, inner), lambda b: (0, 0))],
        out_specs=[pl.BlockSpec((BLK, inner), lambda b: (b, 0)),
                   pl.BlockSpec((1,) + s0t.shape[1:], smap4)],
        out_shape=[jax.ShapeDtypeStruct((m, inner), BF16),
                   jax.ShapeDtypeStruct(s0t.shape, F32)],
        scratch_shapes=[pltpu.VMEM((BLK, inner), F32)],
        compiler_params=_cp(),
        name='ssd_scan',
    )(xc, z, dt, s0t, pad_h(dt_bias), pad_h(a_log), pad_h(d_skip), norm_w.reshape(1, inner))


def _conv_taps(a, prev8, cw_ref, cb, cs, width):
    groups = [prev8] + [a[g * SUBLANES:(g + 1) * SUBLANES] for g in range(a.shape[0] // SUBLANES)]
    first_row = lax.broadcasted_iota(jnp.int32, (SUBLANES, 1), 0) == 0
    w0 = cw_ref[0:1, cs]
    z = [x * w0 for x in groups]
    for j in range(1, width):
        wj = cw_ref[j:j + 1, cs]
        rot = [pltpu.roll(t, 1, 0) for t in z]
        z = [groups[0] * wj + rot[0]] + [groups[g] * wj + jnp.where(first_row, rot[g - 1], rot[g])
                                         for g in range(1, len(groups))]
    return jnp.concatenate(z[1:], axis=0) + cb


def _mm_conv_body(*refs, width, gated, nbp, lreal, dec_seq, tm, chunk):
    if gated:
        x_ref, w_ref, wu_ref, tail_ref, cw_ref, cb_ref, o_ref, st_ref, carry_ref = refs
    else:
        x_ref, w_ref, tail_ref, cw_ref, cb_ref, o_ref, st_ref, carry_ref = refs
    i = pl.program_id(0)
    nsub = tm // BLK
    n = w_ref.shape[1]
    nv_last = lreal - ((lreal - 1) // BLK) * BLK
    x = x_ref[...].astype(BF16)
    row = lax.broadcasted_iota(jnp.int32, (BLK, 1), 0)

    @pl.when(i == 0)
    def _():
        st_ref[...] = jnp.zeros_like(st_ref)
        carry_ref[...] = jnp.zeros_like(carry_ref)

    for c0 in range(0, n, chunk):
        cs = slice(c0, c0 + chunk)
        acc = jnp.dot(x, w_ref[:, cs], preferred_element_type=F32)
        if gated:
            up = jnp.dot(x, wu_ref[:, cs], preferred_element_type=F32)
        for j in range(nsub):
            blk = i * nsub + j
            is_start, nvalid = _block_info(blk, nbp, lreal, dec_seq)
            sid = jnp.maximum(blk - nbp + 1, 0)
            rows = slice(j * BLK, (j + 1) * BLK)
            a = acc[rows]
            before = carry_ref[:, cs] if j == 0 else acc[j * BLK - SUBLANES:j * BLK]
            prev8 = jnp.where(is_start, tail_ref[sid, :, cs], before)
            conv = _conv_taps(a, prev8, cw_ref, cb_ref[:, cs], cs, width)
            act = conv * _sigmoid(conv)
            if gated:
                o_ref[rows, cs] = (act * up[rows]).astype(o_ref.dtype)
            else:
                o_ref[rows, cs] = jnp.where(row < nvalid, act, 0.0).astype(o_ref.dtype)
            both = jnp.concatenate([prev8, a], axis=0)
            cand = jnp.where(blk >= nbp, both[dec_seq:dec_seq + SUBLANES],
                             jnp.where(nvalid == BLK, both[BLK:BLK + SUBLANES], both[nv_last:nv_last + SUBLANES]))
            st_ref[sid, :, cs] = jnp.where(nvalid > 0, cand, st_ref[sid, :, cs])
        carry_ref[:, cs] = acc[tm - SUBLANES:tm]


def _mm_conv(x, w, wu, tails, conv_w, conv_b, out_dtype, lay, name):
    m, k = x.shape
    n = w.shape[1]
    width = conv_w.shape[0]
    nstream = tails.shape[0]
    gated = wu is not None
    body = functools.partial(_mm_conv_body, width=width, gated=gated, nbp=lay['nbp'], lreal=lay['lreal'],
                             dec_seq=lay['dec_seq'], tm=TM, chunk=2 * LANES)
    w_spec = pl.BlockSpec((k, n), lambda i: (0, 0), pipeline_mode=pl.Buffered(1))
    full = lambda shape: pl.BlockSpec(shape, lambda i: (0,) * len(shape))
    return pl.pallas_call(
        body,
        grid=(m // TM,),
        in_specs=[pl.BlockSpec((TM, k), lambda i: (i, 0)), w_spec] + ([w_spec] if gated else [])
        + [full((nstream, SUBLANES, n)), full((width, n)), full((1, n))],
        out_specs=[pl.BlockSpec((TM, n), lambda i: (i, 0)), full((nstream, SUBLANES, n))],
        out_shape=[jax.ShapeDtypeStruct((m, n), out_dtype), jax.ShapeDtypeStruct((nstream, SUBLANES, n), F32)],
        scratch_shapes=[pltpu.VMEM((SUBLANES, n), F32)],
        compiler_params=_cp(),
        name=name,
    )(*([x, w] + ([wu] if gated else []) + [tails, conv_w, conv_b.reshape(1, n)]))


def _qkv_rope_body(x_ref, w_ref, cos_ref, sin_ref, q16_ref, k32_ref, k16_ref, v32_ref, v16_ref, vt_ref, *,
                   d_model, scale):
    cos = cos_ref[...]
    sin = sin_ref[...]
    lane = lax.broadcasted_iota(jnp.int32, cos.shape, 1)
    first_half = (lane % DA_HD) < (DA_HD // 2)
    x = x_ref[...].astype(BF16)
    wide = 2 * LANES
    for c0 in range(0, 3 * d_model, wide):
        acc = jnp.dot(x, w_ref[:, c0:c0 + wide], preferred_element_type=F32)
        for half in range(wide // LANES):
            col = c0 + half * LANES
            t = acc[:, half * LANES:(half + 1) * LANES]
            if col < 2 * d_model:
                partner = jnp.where(first_half, -pltpu.roll(t, LANES - DA_HD // 2, 1), pltpu.roll(t, DA_HD // 2, 1))
                out = t * cos + partner * sin
                if col < d_model:
                    q16_ref[:, col:col + LANES] = (out * scale).astype(BF16)
                else:
                    k32_ref[:, col - d_model:col - d_model + LANES] = out
                    k16_ref[:, col - d_model:col - d_model + LANES] = out.astype(BF16)
            else:
                vc = col - 2 * d_model
                v32_ref[:, vc:vc + LANES] = t
                v16_ref[:, vc:vc + LANES] = t.astype(BF16)
                vt_ref[vc // LANES, 0] = t.T.astype(BF16)


def _qkv_rope(x, w_qkv, cos, sin):
    m, d_model = x.shape
    t = ATT_TK
    nh = d_model // LANES
    rows = lambda dt: (pl.BlockSpec((t, d_model), lambda i: (i, 0)), jax.ShapeDtypeStruct((m, d_model), dt))
    outs = [rows(BF16), rows(F32), rows(BF16), rows(F32), rows(BF16),
            (pl.BlockSpec((nh, 1, LANES, t), lambda i: (0, i, 0, 0)),
             jax.ShapeDtypeStruct((nh, m // t, LANES, t), BF16))]
    return pl.pallas_call(
        functools.partial(_qkv_rope_body, d_model=d_model, scale=DA_HD ** -0.5 * math.log2(math.e)),
        grid=(m // t,),
        in_specs=[pl.BlockSpec((t, d_model), lambda i: (i, 0)),
                  pl.BlockSpec(w_qkv.shape, lambda i: (0, 0), pipeline_mode=pl.Buffered(1)),
                  pl.BlockSpec((t, LANES), lambda i: (i, 0)),
                  pl.BlockSpec((t, LANES), lambda i: (i, 0))],
        out_specs=[o[0] for o in outs],
        out_shape=[o[1] for o in outs],
        compiler_params=_cp(),
        name='da_qkv_rope',
    )(x, w_qkv, cos, sin)


def _diff_lambda(lam_ref, lam_init):
    lp = lam_ref[...]
    return (jnp.exp(jnp.sum(lp[0:1] * lp[1:2], axis=-1, keepdims=True))
            - jnp.exp(jnp.sum(lp[2:3] * lp[3:4], axis=-1, keepdims=True)) + lam_init)


def _attn_prompt_body(*refs, tq, lp, **kw):
    o_ref = refs[5]

    @pl.when(pl.program_id(1) < lp // tq)
    def _():
        _attn_prompt_tile(*refs, tq=tq, lp=lp, **kw)

    @pl.when(pl.program_id(1) >= lp // tq)
    def _():
        o_ref[...] = jnp.zeros_like(o_ref)


def _attn_prompt_tile(q_ref, k_ref, vt_ref, lam_ref, g_ref, o_ref, acc_ref, sa_ref, sb_ref, *, tq, tk, lp, lreal, off,
                      lam_init):
    qi = pl.program_id(1)
    nsub = tq // ATT_TQ_SUB
    chains = [(a, c) for a in range(nsub) for c in range(2)]
    qts, qchs = {}, {}
    for a in range(nsub):
        qt = q_ref[a * ATT_TQ_SUB:(a + 1) * ATT_TQ_SUB, :].astype(F32).T
        feat = lax.broadcasted_iota(jnp.int32, qt.shape, 0)
        qts[a, 0] = jnp.where(feat < DA_HD, qt, 0.0).astype(BF16)
        qts[a, 1] = jnp.where(feat >= DA_HD, qt, 0.0).astype(BF16)
        qpos = qi * tq + a * ATT_TQ_SUB + lax.broadcasted_iota(jnp.int32, (1, ATT_TQ_SUB), 1)
        qchs[a] = (qpos + off) >> 6
    acc_ref[...] = jnp.zeros_like(acc_ref)
    full_last = jnp.minimum(((qi * tq + off) >> 6) * CHUNK - off + CHUNK - 1, lreal - 1)
    any_last = jnp.minimum(((qi * tq + tq - 1 + off) >> 6) * CHUNK - off + CHUNK - 1, lreal - 1)
    nk = any_last // tk + 1
    nfull = jnp.minimum((full_last + 1) // tk, nk - 1)

    def scores(ki, s_ref, width=tk):
        kt = k_ref[pl.ds(pl.multiple_of(ki * tk, tk), width), :]
        for n, ch in enumerate(chains):
            s_ref[n, 0:width, :] = jnp.dot(kt, qts[ch], preferred_element_type=F32)

    def consume(ki, s_ref, carry, masked, width=tk):
        vt = vt_ref[0, ki, :, 0:width]
        s = [s_ref[n, 0:width, :] for n in range(len(chains))]
        if masked:
            kpos = ki * tk + lax.broadcasted_iota(jnp.int32, (width, 1), 0)
            for n, (a, _) in enumerate(chains):
                vis = jnp.logical_and(((kpos + off) >> 6) <= qchs[a], kpos < lreal)
                s[n] = jnp.where(vis, s[n], NEG_BIG)
        m_new = [jnp.maximum(carry[2 * n], jnp.max(s[n], axis=0, keepdims=True)) for n in range(len(chains))]
        p = [jnp.exp2(s[n] - m_new[n]) for n in range(len(chains))]
        new = []
        for n in range(len(chains)):
            alpha = jnp.exp2(carry[2 * n] - m_new[n])
            l_new = alpha * carry[2 * n + 1] + jnp.sum(p[n], axis=0, keepdims=True)
            acc_ref[n] = alpha * acc_ref[n] + jnp.dot(vt, p[n].astype(BF16), preferred_element_type=F32)
            new += [m_new[n], l_new]
        return tuple(new)

    npair = nfull // 2
    last_tile = nk - 1

    def pair(j, carry):
        scores(2 * j + 1, sb_ref)
        carry = consume(2 * j, sa_ref, carry, masked=False)
        scores(jnp.minimum(2 * j + 2, last_tile), sa_ref)
        return consume(2 * j + 1, sb_ref, carry, masked=False)

    def single(ki, carry, width=tk):
        scores(ki, sb_ref, width)
        return consume(ki, sb_ref, carry, True, width)

    carry = (jnp.full((1, ATT_TQ_SUB), NEG_BIG, F32), jnp.zeros((1, ATT_TQ_SUB), F32)) * len(chains)
    scores(0, sa_ref)
    carry = lax.fori_loop(0, npair, pair, carry)
    first_left = 2 * npair
    carry = lax.cond(first_left < last_tile, lambda c: consume(first_left, sa_ref, c, True), lambda c: c, carry)
    carry = lax.fori_loop(first_left + 1, last_tile, single, carry)
    narrow = any_last - last_tile * tk < ATT_TK_TAIL
    carry = lax.cond(narrow, functools.partial(single, last_tile, width=ATT_TK_TAIL),
                     functools.partial(single, last_tile), carry)
    lam = _diff_lambda(lam_ref, lam_init)
    for a in range(nsub):
        l0, l1 = carry[4 * a + 1], carry[4 * a + 3]
        o = acc_ref[2 * a] / l0 - lam * (acc_ref[2 * a + 1] / l1)
        ms = jnp.mean(o * o, axis=0, keepdims=True)
        o = o * lax.rsqrt(ms + SUBLN_EPS) * g_ref[...] * (1.0 - lam_init)
        o_ref[a * ATT_TQ_SUB:(a + 1) * ATT_TQ_SUB, :] = o.T.astype(o_ref.dtype)


def _attn_sample_body(q_ref, kn_ref, vn_ref, kc_ref, vc_ref, lam_ref, g_ref, o_alias_ref, o_ref, acc_ref, *,
                      tk, past, dec_seq, lam_init):
    del o_alias_ref
    nq = -(-dec_seq // 16) * 16
    q = q_ref[0:nq, :]
    lane = lax.broadcasted_iota(jnp.int32, q.shape, 1)
    zero = jnp.zeros_like(q)
    q2 = jnp.concatenate([jnp.where(lane < DA_HD, q, zero), jnp.where(lane >= DA_HD, q, zero)], axis=0)
    acc_ref[...] = jnp.zeros_like(acc_ref)
    qrow = lax.broadcasted_iota(jnp.int32, (2 * nq, 1), 0)
    qch = (past + jnp.where(qrow >= nq, qrow - nq, qrow)) >> 6

    def tile(kt, vt, vis, carry):
        m_old, l_old = carry
        s = lax.dot_general(q2, kt, _NT, preferred_element_type=F32)
        s = jnp.where(vis, s, NEG_BIG)
        m_new = jnp.maximum(m_old, jnp.max(s, axis=-1, keepdims=True))
        p = jnp.exp2(s - m_new)
        alpha = jnp.exp2(m_old - m_new)
        l_new = alpha * l_old + jnp.sum(p, axis=-1, keepdims=True)
        acc_ref[...] = alpha * acc_ref[...] + jnp.dot(p.astype(BF16), vt, preferred_element_type=F32)
        return m_new, l_new

    def body(ki, c):
        ks = pl.multiple_of(ki * tk, tk)
        kt = kc_ref[0, pl.ds(ks, tk), :].astype(BF16)
        vt = vc_ref[0, pl.ds(ks, tk), :].astype(BF16)
        kpos = ks + lax.broadcasted_iota(jnp.int32, (1, tk), 1)
        return tile(kt, vt, (kpos >> 6) <= qch, c)

    carry = (jnp.full((2 * nq, 1), NEG_BIG, F32), jnp.zeros((2 * nq, 1), F32))
    carry = lax.fori_loop(0, past // tk, body, carry)
    kidx = lax.broadcasted_iota(jnp.int32, (1, nq), 1)
    vis = jnp.logical_and(((past + kidx) >> 6) <= qch, kidx < dec_seq)
    _, l = tile(kn_ref[0:nq, :], vn_ref[0:nq, :], vis, carry)
    lam = _diff_lambda(lam_ref, lam_init)
    on = acc_ref[...] / l
    o = on[:nq] - lam * on[nq:]
    ms = jnp.mean(o * o, axis=-1, keepdims=True)
    o_ref[...] = jnp.zeros_like(o_ref)
    o_ref[0:nq, :] = (o * lax.rsqrt(ms + SUBLN_EPS) * g_ref[...] * (1.0 - lam_init)).astype(o_ref.dtype)


def _attention(q16, k16, v16, vt16, k_cache, v_cache, lam_p, subln_g, lam_init, lay):
    m, d = q16.shape
    nh = d // LANES
    lp = lay['lp']
    ns = lay['ns']
    past = k_cache.shape[1]
    g = subln_g.astype(F32)
    lam_p = lam_p.astype(F32)
    nkt = vt16.shape[1]
    o_prompt = pl.pallas_call(
        functools.partial(_attn_prompt_body, tq=ATT_TQ, tk=ATT_TK, lp=lp, lreal=lay['lreal'], off=CHUNK - N_META,
                          lam_init=lam_init),
        grid=(nh, m // ATT_TQ),
        in_specs=[pl.BlockSpec((ATT_TQ, LANES), lambda h, i: (i, h)),
                  pl.BlockSpec((lp, LANES), lambda h, i: (0, h)),
                  pl.BlockSpec((1, nkt, LANES, ATT_TK), lambda h, i: (h, 0, 0, 0)),
                  pl.BlockSpec((4, DA_HD), lambda h, i: (0, 0)),
                  pl.BlockSpec((LANES, 1), lambda h, i: (0, 0))],
        out_specs=pl.BlockSpec((ATT_TQ, LANES), lambda h, i: (i, h)),
        out_shape=jax.ShapeDtypeStruct((m, d), BF16),
        scratch_shapes=[pltpu.VMEM((2 * ATT_TQ // ATT_TQ_SUB, LANES, ATT_TQ_SUB), F32),
                        pltpu.VMEM((2 * ATT_TQ // ATT_TQ_SUB, ATT_TK, ATT_TQ_SUB), F32),
                        pltpu.VMEM((2 * ATT_TQ // ATT_TQ_SUB, ATT_TK, ATT_TQ_SUB), F32)],
        compiler_params=_cp(),
        name='attn_prompt',
    )(q16, k16, vt16, lam_p, g.reshape(LANES, 1))
    nbp = lay['nbp']
    tk = min(ATT_TK_SAMPLE, past)
    nq = -(-lay['dec_seq'] // 16) * 16
    blk_spec = pl.BlockSpec((BLK, LANES), lambda s, h: (nbp + s, h))
    return pl.pallas_call(
        functools.partial(_attn_sample_body, tk=tk, past=past, dec_seq=lay['dec_seq'], lam_init=lam_init),
        grid=(ns, nh),
        in_specs=[blk_spec, blk_spec, blk_spec,
                  pl.BlockSpec((1, past, LANES), lambda s, h: (s, 0, h)),
                  pl.BlockSpec((1, past, LANES), lambda s, h: (s, 0, h)),
                  pl.BlockSpec((4, DA_HD), lambda s, h: (0, 0)),
                  pl.BlockSpec((1, LANES), lambda s, h: (0, 0)),
                  pl.BlockSpec(memory_space=pl.ANY)],
        out_specs=blk_spec,
        out_shape=jax.ShapeDtypeStruct((m, d), BF16),
        scratch_shapes=[pltpu.VMEM((2 * nq, LANES), F32)],
        input_output_aliases={7: 0},
        compiler_params=_cp(),
        name='attn_sample',
    )(q16, k16, v16, k_cache, v_cache, lam_p, g.reshape(1, LANES), o_prompt)


def _rk_in_body(x_ref, tail_ref, mu_ref, wr_ref, wk_ref, wv_ref, w1_ref, w2_ref, a1_ref, a2_ref, g1_ref, g2_ref,
                r_ref, k_ref, v_ref, lw_ref, la_ref, g_ref, carry_ref, *, nbp, lreal, dec_seq, tm):
    i = pl.program_id(0)

    @pl.when(i == 0)
    def _():
        carry_ref[...] = jnp.zeros_like(carry_ref)

    x = x_ref[...]
    shifted = []
    for j in range(tm // BLK):
        blk = i * (tm // BLK) + j
        is_start, _ = _block_info(blk, nbp, lreal, dec_seq)
        sid = jnp.maximum(blk - nbp + 1, 0)
        before = carry_ref[...] if j == 0 else x[j * BLK - SUBLANES:j * BLK]
        prev8 = jnp.where(is_start, tail_ref[sid], before)
        ext = jnp.concatenate([prev8, x[j * BLK:(j + 1) * BLK]], axis=0)
        shifted.append(ext[SUBLANES - 1:SUBLANES - 1 + BLK])
    carry_ref[...] = x[tm - SUBLANES:tm]
    xx = jnp.concatenate(shifted, axis=0) - x

    def mix(n):
        return (x + xx * mu_ref[n:n + 1, :]).astype(BF16)

    def mm(a, w_ref):
        return jnp.dot(a, w_ref[...], preferred_element_type=F32)

    r_ref[...] = mm(mix(0), wr_ref)
    lw_ref[...] = mm(jnp.tanh(mm(mix(1), w1_ref)).astype(BF16), w2_ref)
    k_ref[...] = mm(mix(2), wk_ref)
    v_ref[...] = mm(mix(3), wv_ref)
    la_ref[...] = mm(mm(mix(4), a1_ref).astype(BF16), a2_ref)
    g_ref[...] = mm(_sigmoid(mm(mix(5), g1_ref)).astype(BF16), g2_ref)


def _rk_in(x, tails, mu, w_r, w_k, w_v, w1, w2, a1, a2, g1, g2, lay):
    m, d = x.shape
    nstream = tails.shape[0]

    def lora_pair(u, w):
        r = u.shape[1]
        rp = -(-r // LANES) * LANES
        return jnp.pad(u, ((0, 0), (0, rp - r))).astype(BF16), jnp.pad(w, ((0, rp - r), (0, 0))).astype(BF16)

    ws = [w_r.astype(BF16), w_k.astype(BF16), w_v.astype(BF16), *lora_pair(w1, w2), *lora_pair(a1, a2),
          *lora_pair(g1, g2)]
    full = lambda a: pl.BlockSpec(a.shape, lambda i: (0,) * a.ndim, pipeline_mode=pl.Buffered(1))
    row = pl.BlockSpec((TM, d), lambda i: (i, 0))
    body = functools.partial(_rk_in_body, nbp=lay['nbp'], lreal=lay['lreal'], dec_seq=lay['dec_seq'], tm=TM)
    return pl.pallas_call(
        body,
        grid=(m // TM,),
        in_specs=[row, full(tails), full(mu)] + [full(w) for w in ws],
        out_specs=[row] * 6,
        out_shape=[jax.ShapeDtypeStruct((m, d), F32)] * 6,
        scratch_shapes=[pltpu.VMEM((SUBLANES, d), F32)],
        compiler_params=_cp(),
        name='rk_in',
    )(x, tails, mu, *ws)


def _half_sum(x, low):
    s_lo = jnp.sum(jnp.where(low, x, 0.0), axis=-1, keepdims=True)
    s_hi = jnp.sum(jnp.where(low, 0.0, x), axis=-1, keepdims=True)
    return jnp.where(low, s_lo, s_hi)


def _wkv_body(r_ref, k_ref, v_ref, lw_ref, la_ref, g_ref, s0_ref,
              w0_ref, a0_ref, kk_ref, ka_ref, lng_ref, lnb_ref, rk_ref,
              o_ref, st_ref, stage_ref, yb_ref, pw_ref, x_ref, ak_ref, dd_ref, es_ref, bke_ref, gc_ref,
              *, nbp, lreal, dec_seq):
    c = BLK
    npair = st_ref.shape[1]
    half = c // 2
    b = pl.program_id(0)
    is_start, nvalid = _block_info(b, nbp, lreal, dec_seq)

    @pl.when(is_start)
    def _():
        st_ref[0] = s0_ref[0]

    for i, ref in enumerate((r_ref, k_ref, v_ref, lw_ref, la_ref, g_ref)):
        for p in range(npair):
            stage_ref[i, p] = ref[:, p * LANES:(p + 1) * LANES]

    row = lax.broadcasted_iota(jnp.int32, (c, 1), 0)
    valid = row < nvalid
    lane = lax.broadcasted_iota(jnp.int32, (c, LANES), 1)
    low = lane < RK_HD
    ri = lax.broadcasted_iota(jnp.int32, (c, c), 0)
    ci = lax.broadcasted_iota(jnp.int32, (c, c), 1)
    lower_incl = ri >= ci
    lower_strict = ri > ci
    tri = lower_incl.astype(F32)
    eye = (ri == ci).astype(F32)
    same_head = (ri < RK_HD) == (ci < RK_HD)

    prep_group = 8

    def each(f, *cols):
        return [f(*args) for args in zip(*cols)]

    def prep(g, carry):
        ps = [g * prep_group + i for i in range(prep_group)]
        r = [stage_ref[0, p] for p in ps]
        k = [stage_ref[1, p] for p in ps]
        v = [jnp.where(valid, stage_ref[2, p], 0.0) for p in ps]
        w_log = [-_softplus(-(w0_ref[p] + stage_ref[3, p])) - 0.5 for p in ps]
        logw = each(lambda t: jnp.where(valid, -jnp.exp(t), 0.0), w_log)
        ag = [_sigmoid(a0_ref[p] + stage_ref[4, p]) for p in ps]
        kk = [ki * kk_ref[p] for ki, p in zip(k, ps)]
        kk = each(lambda t: t / jnp.maximum(jnp.sqrt(_half_sum(t * t, low)), 1e-12), kk)
        kxm = [jnp.where(valid, ki * (1.0 + (gi - 1.0) * ka_ref[p]), 0.0) for ki, gi, p in zip(k, ag, ps)]
        a = each(lambda t: jnp.where(valid, -t, 0.0), kk)
        bv = each(lambda t, gi: jnp.where(valid, t * gi, 0.0), kk, ag)

        cum = each(lambda t: jnp.dot(tri, t, preferred_element_type=F32, precision=lax.Precision.HIGHEST), logw)
        e_cum = each(jnp.exp, cum)
        e_prev = each(lambda ci_, li: jnp.exp(ci_ - li), cum, logw)
        e_neg = each(lambda ci_: jnp.exp(-ci_), cum)
        mid_dn = each(lambda ci_: jnp.exp(-ci_[half - 1:half, :]), cum)
        mid_up = each(lambda ci_: jnp.exp(ci_[half - 1:half, :]), cum)
        end_up = each(lambda ci_: jnp.exp(ci_[c - 1:c, :]), cum)
        a_abs = each(lambda x_, e: x_ * e, a, e_prev)
        r_abs = each(lambda x_, e: x_ * e, r, e_cum)
        bn = each(lambda x_, e: x_ * e, bv, e_neg)
        kn = each(lambda x_, e: x_ * e, kxm, e_neg)

        for i, p in enumerate(ps):
            stage_ref[1, p] = kxm[i]
            stage_ref[2, p] = v[i]
            bke_ref[p] = jnp.concatenate([bn[i] * end_up[i], kn[i] * end_up[i]], axis=0).astype(BF16)
            gc_ref[p] = jnp.broadcast_to(end_up[i], (SUBLANES, LANES))
        for i, p in enumerate(ps):
            s = st_ref[0, p]
            es_ref[p] = _bdot_nt(jnp.concatenate([a_abs[i], r_abs[i]], axis=0), s)
        for i, p in enumerate(ps):
            a_rel = a_abs[i] * mid_dn[i]
            r_rel = r_abs[i] * mid_dn[i]
            bk_rel = jnp.concatenate([bn[i] * mid_up[i], kn[i] * mid_up[i]], axis=0).astype(BF16)
            for j, sel in enumerate((low, jnp.logical_not(low))):
                ar = jnp.concatenate([jnp.where(sel, a_rel, 0.0), jnp.where(sel, r_rel, 0.0)], axis=0)
                mm = lax.dot_general(ar.astype(BF16), bk_rel, _NT, preferred_element_type=F32)
                n = jnp.where(lower_strict, mm[:c, :c], 0.0)
                h = 2 * p + j
                pw_ref[h] = n.astype(BF16)
                x_ref[h] = eye + n
                ak_ref[h] = jnp.where(lower_strict, mm[:c, c:], 0.0).astype(BF16)
                dd_ref[h] = jnp.concatenate([jnp.where(lower_incl, mm[c:, :c], 0.0),
                                             jnp.where(lower_incl, mm[c:, c:], 0.0)], axis=1).astype(BF16)
        return carry

    lax.fori_loop(0, npair // prep_group, prep, 0)

    nlev = int(math.log2(c))
    inv_group = 16

    def invert(g, carry):
        hs = [g * inv_group + j for j in range(inv_group)]
        for h in hs:
            pw = pw_ref[h]
            pw_ref[h] = jnp.dot(pw, pw, preferred_element_type=F32).astype(BF16)
        for _ in range(1, nlev - 1):
            for h in hs:
                pw = pw_ref[h]
                both = jnp.dot(jnp.concatenate([pw, x_ref[h].astype(BF16)], axis=0), pw,
                               preferred_element_type=F32)
                pw_ref[h] = both[:c].astype(BF16)
                x_ref[h] = x_ref[h] + both[c:]
        for h in hs:
            x = x_ref[h]
            x_ref[h] = x + jnp.dot(x.astype(BF16), pw_ref[h], preferred_element_type=F32)
        return carry

    lax.fori_loop(0, 2 * npair // inv_group, invert, 0)

    out_group = 8

    def emit(g, carry):
        ps = [g * out_group + i for i in range(out_group)]
        hs = [(i, j) for i in range(out_group) for j in range(2)]
        rhs, us, ys = {}, {}, {}
        for i, j in hs:
            p = ps[i]
            rhs[i, j] = es_ref[p, 0:c, :] + jnp.dot(ak_ref[2 * p + j], stage_ref[2, p].astype(BF16),
                                                    preferred_element_type=F32)
        for i, j in hs:
            us[i, j] = _bdot(x_ref[2 * ps[i] + j], rhs[i, j])
        for i, j in hs:
            p = ps[i]
            uv = jnp.concatenate([us[i, j], stage_ref[2, p]], axis=0).astype(BF16)
            ys[i, j] = es_ref[p, c:2 * c, :] + jnp.dot(dd_ref[2 * p + j], uv, preferred_element_type=F32)
        for i, p in enumerate(ps):
            v = stage_ref[2, p]
            u = jnp.where(low, us[i, 0], us[i, 1])
            y = jnp.where(low, ys[i, 0], ys[i, 1])
            uv_t = jnp.concatenate([u, v], axis=0).T.astype(BF16)
            upd = jnp.dot(uv_t, bke_ref[p], preferred_element_type=F32)
            st_ref[0, p] = st_ref[0, p] * gc_ref[p, 0:1, :] + jnp.where(same_head, upd, 0.0)

            mean = _half_sum(y, low) * (1.0 / RK_HD)
            dy = y - mean
            var = _half_sum(dy * dy, low) * (1.0 / RK_HD)
            yn = dy * lax.rsqrt(var + RK_GN_EPS) * lng_ref[p] + lnb_ref[p]
            bonus = _half_sum(stage_ref[0, p] * stage_ref[1, p] * rk_ref[p], low) * v
            yb_ref[p] = ((yn + bonus) * stage_ref[5, p]).astype(yb_ref.dtype)
        return carry

    lax.fori_loop(0, npair // out_group, emit, 0)
    for p in range(npair):
        o_ref[:, p * LANES:(p + 1) * LANES] = yb_ref[p]


def _wkv(r, k, v, lw, la, gate, s0, w0, a0, k_k, k_a, lnx_g, lnx_b, r_k, lay):
    m, d = r.shape
    npair = d // LANES

    def prm(t):
        return t.astype(F32).reshape(npair, 1, LANES)

    body = functools.partial(_wkv_body, nbp=lay['nbp'], lreal=lay['lreal'], dec_seq=lay['dec_seq'])
    row_spec = pl.BlockSpec((BLK, d), lambda b: (b, 0))
    prm_spec = pl.BlockSpec((npair, 1, LANES), lambda b: (0, 0, 0))
    smap = _stream_map(lay['nbp'], 4)
    return pl.pallas_call(
        body,
        grid=(m // BLK,),
        in_specs=[row_spec] * 6 + [pl.BlockSpec((1,) + s0.shape[1:], smap)] + [prm_spec] * 7,
        out_specs=[row_spec, pl.BlockSpec((1,) + s0.shape[1:], smap)],
        out_shape=[jax.ShapeDtypeStruct((m, d), BF16), jax.ShapeDtypeStruct(s0.shape, F32)],
        scratch_shapes=[pltpu.VMEM((6, npair, BLK, LANES), F32),
                        pltpu.VMEM((npair, BLK, LANES), BF16),
                        pltpu.VMEM((2 * npair, BLK, BLK), BF16),
                        pltpu.VMEM((2 * npair, BLK, BLK), F32),
                        pltpu.VMEM((2 * npair, BLK, BLK), BF16),
                        pltpu.VMEM((2 * npair, BLK, 2 * BLK), BF16),
                        pltpu.VMEM((npair, 2 * BLK, LANES), F32),
                        pltpu.VMEM((npair, 2 * BLK, LANES), BF16),
                        pltpu.VMEM((npair, SUBLANES, LANES), F32)],
        compiler_params=_cp(),
        name='wkv7',
    )(r, k, v, lw, la, gate, s0, prm(w0), prm(a0), prm(k_k), prm(k_a), prm(lnx_g), prm(lnx_b), prm(r_k))


def _tails(prev, nstream):
    ns, w, c = prev.shape
    t = jnp.zeros((nstream, SUBLANES, c), F32)
    return t.at[1:, SUBLANES - w:, :].set(prev.astype(F32))


def _last_rows(flat, w, lay):
    lreal, lp, ns, dec = lay['lreal'], lay['lp'], lay['ns'], lay['dec_seq']
    p = flat[lreal - w:lreal][None]
    s = flat[lp:].reshape(ns, BLK, -1)[:, dec - w:dec]
    return p, s


def kernel(x_prompt, x_sample, cache_attn_k, cache_attn_v, state_ssm, state_ssd_conv, state_wkv, state_rwkv_shift, state_ffn_conv, meta_tokens, ssd_w_in, ssd_conv_w, ssd_conv_b, ssd_dt_bias, ssd_a_log, ssd_d, ssd_norm_w, ssd_w_out, da_w_qkv, da_lambda, da_subln_g, da_w_o, rk_mu, rk_w0, rk_w1, rk_w2, rk_a0, rk_a1, rk_a2, rk_g1, rk_g2, rk_k_k, rk_k_a, rk_r_k, rk_w_r, rk_w_k, rk_w_v, rk_w_o, rk_lnx_g, rk_lnx_b, ffn_w_up, ffn_w_gate, ffn_conv_w, ffn_conv_b, ffn_w_down, ln_g, ln_b):
    bp, seq, d = x_prompt.shape
    ns, dec_seq, _ = x_sample.shape
    depth = ln_g.shape[0]
    past = cache_attn_k.shape[2]
    assert bp == 1 and dec_seq >= SSD_CONV - 1 and dec_seq <= BLK and past % min(ATT_TK_SAMPLE, past) == 0
    lreal = N_META + seq
    row_align = max(TM, ATT_TQ, ATT_TK)
    lp = -(-lreal // row_align) * row_align
    m = lp + ns * BLK
    assert m % row_align == 0
    nstream = ns + 1
    lay = dict(lreal=lreal, lp=lp, nbp=lp // BLK, ns=ns, dec_seq=dec_seq)
    alpha = (2 * depth) ** 0.25

    x = jnp.concatenate([
        meta_tokens.astype(F32), x_prompt[0], jnp.zeros((lp - lreal, d), F32),
        jnp.pad(x_sample, ((0, 0), (0, BLK - dec_seq), (0, 0))).reshape(ns * BLK, d)], axis=0)

    pos = jnp.concatenate([jnp.arange(lp), jnp.tile(past + jnp.arange(BLK), ns)]).astype(F32)
    half = DA_HD // 2
    inv = ROPE_THETA ** (-jnp.arange(half, dtype=F32) / half)
    ang = pos[:, None] * inv[None, :]
    cos_t = jnp.tile(jnp.cos(ang), (1, LANES // half))
    sin_t = jnp.tile(jnp.sin(ang), (1, LANES // half))

    inner = ssd_norm_w.shape[1]
    nheads = ssd_dt_bias.shape[1]
    hpg = nheads // SSD_GROUPS
    new = {n: [] for n in ('k', 'v', 'ssm', 'ssd_conv', 'wkv', 'shift', 'ffn_conv')}

    for i in range(depth):
        j, kind = i // N_MIXERS, i % N_MIXERS
        if kind == 0:
            w_in = ssd_w_in[j]
            cdim = ssd_conv_w.shape[2]
            w_z = w_in[:, :inner].astype(BF16)
            w_xbc = w_in[:, inner:inner + cdim].astype(BF16)
            w_dt = jnp.pad(w_in[:, inner + cdim:], ((0, 0), (0, LANES - nheads))).astype(BF16)
            z, dt = _mm_multi(x, [w_z, w_dt], [F32, F32], 'ssd_in_z_dt')
            xc, conv_rows = _mm_conv(x, w_xbc, None, _tails(state_ssd_conv[j], nstream), ssd_conv_w[j],
                                     ssd_conv_b[j], F32, lay, 'ssd_in_xbc_conv')
            s0 = jnp.concatenate([jnp.zeros((1,) + state_ssm.shape[2:], F32), state_ssm[j].astype(F32)], axis=0)
            s0t = s0.reshape(nstream, SSD_GROUPS, hpg * SSD_HEADDIM, SSD_STATE).swapaxes(2, 3)
            y, st = _ssd_scan(xc, z, dt, s0t, ssd_dt_bias[j], ssd_a_log[j], ssd_d[j], ssd_norm_w[j], lay)
            w_o = ssd_w_out[j].astype(BF16)
            conv_rows = conv_rows[:, SUBLANES - (SSD_CONV - 1):]
            new['ssd_conv'].append((conv_rows[:1], conv_rows[1:]))
            ssm = st.swapaxes(2, 3).reshape(nstream, nheads, SSD_HEADDIM, SSD_STATE)
            new['ssm'].append((ssm[:1], ssm[1:]))
        elif kind == 1:
            lam_init = 0.8 - 0.6 * math.exp(-0.3 * i)
            q16, k32, k16, v32, v16, vt16 = _qkv_rope(x, da_w_qkv[j].astype(BF16), cos_t, sin_t)
            nh = d // (2 * DA_HD)
            y = _attention(q16, k16, v16, vt16, cache_attn_k[j].reshape(ns, past, d),
                           cache_attn_v[j].reshape(ns, past, d), da_lambda[j], da_subln_g[j], lam_init, lay)
            w_o = da_w_o[j].astype(BF16)
            kp, ks = k32[:lreal][None], k32[lp:].reshape(ns, BLK, d)[:, :dec_seq]
            vp, vs = v32[:lreal][None], v32[lp:].reshape(ns, BLK, d)[:, :dec_seq]
            new['k'].append((kp.reshape(1, lreal, nh, 2, DA_HD), ks.reshape(ns, dec_seq, nh, 2, DA_HD)))
            new['v'].append((vp.reshape(1, lreal, nh, 2 * DA_HD), vs.reshape(ns, dec_seq, nh, 2 * DA_HD)))
        else:
            r, k, v, lw, la, gate = _rk_in(x, _tails(state_rwkv_shift[j], nstream), rk_mu[j], rk_w_r[j], rk_w_k[j],
                                           rk_w_v[j], rk_w1[j], rk_w2[j], rk_a1[j], rk_a2[j], rk_g1[j], rk_g2[j], lay)
            nrh = d // RK_HD
            npair = nrh // 2
            sw = jnp.concatenate([jnp.zeros((1,) + state_wkv.shape[2:], F32), state_wkv[j].astype(F32)], axis=0)
            sw = sw.reshape(nstream, npair, 2, RK_HD, RK_HD)
            zero = jnp.zeros_like(sw[:, :, 0])
            s0 = jnp.concatenate([jnp.concatenate([sw[:, :, 0], zero], axis=-1),
                                  jnp.concatenate([zero, sw[:, :, 1]], axis=-1)], axis=-2)
            y, st = _wkv(r, k, v, lw, la, gate, s0, rk_w0[j], rk_a0[j], rk_k_k[j], rk_k_a[j], rk_lnx_g[j],
                         rk_lnx_b[j], rk_r_k[j].reshape(-1), lay)
            w_o = rk_w_o[j].astype(BF16)
            wkv = jnp.stack([st[:, :, :RK_HD, :RK_HD], st[:, :, RK_HD:, RK_HD:]], axis=2).reshape(
                nstream, nrh, RK_HD, RK_HD)
            new['wkv'].append((wkv[:1], wkv[1:]))
            new['shift'].append(_last_rows(x, 1, lay))
        x = _mm_ln(y, w_o, x, ln_g[i, 0], ln_b[i, 0], alpha, 'mixer_out_ln')
        h, gate_rows = _mm_conv(x, ffn_w_gate[i].astype(BF16), ffn_w_up[i].astype(BF16),
                                _tails(state_ffn_conv[i], nstream), ffn_conv_w[i], ffn_conv_b[i], BF16, lay, 'ffn_in_conv')
        gate_rows = gate_rows[:, SUBLANES - (FFN_CONV - 1):]
        new['ffn_conv'].append((gate_rows[:1], gate_rows[1:]))
        x = _mm_ln(h, ffn_w_down[i].astype(BF16), x, ln_g[i, 1], ln_b[i, 1], alpha, 'ffn_out_ln')

    y_prompt = x[N_META:lreal][None]
    y_sample = x[lp:].reshape(ns, BLK, d)[:, :dec_seq]
    order = ('k', 'v', 'ssm', 'ssd_conv', 'wkv', 'shift', 'ffn_conv')
    prompt_states = tuple(jnp.stack([t[0] for t in new[n]]) for n in order)
    sample_states = tuple(jnp.stack([t[1] for t in new[n]]) for n in order)
    return (y_prompt, y_sample) + prompt_states + sample_states
```
